```python
import jax, jax.numpy as jnp
from jax import lax
import numpy as np

D_MODEL = 2048
BATCH = 4
SEQ = 4096
DEPTH = 1

EPS = 1e-6
A_HEADS = 16
A_KV_HEADS = 4
A_HEAD_DIM = 64
A_WIDTH = A_HEADS * A_HEAD_DIM
A_KV_WIDTH = A_KV_HEADS * A_HEAD_DIM
WINDOW = 128
BLOCK = 128
ROT_DIM = A_HEAD_DIM // 4
ROPE_THETA = 500000.0
B_HEADS = 8
B_KEY_DIM = 128
B_VAL_DIM = 128
B_KEY_WIDTH = B_HEADS * B_KEY_DIM
B_WIDTH = B_HEADS * B_VAL_DIM
CHUNK = 64
FF_DIM = 5632
PLE_DIM = 256
SPLITS = (A_WIDTH, A_KV_WIDTH, A_KV_WIDTH, B_KEY_WIDTH, B_KEY_WIDTH, B_WIDTH, B_WIDTH, D_MODEL, D_MODEL)
IN_DIM = A_WIDTH + 2 * A_KV_WIDTH + 2 * B_KEY_WIDTH + 2 * B_WIDTH + 2 * D_MODEL

kernel_name = "hybrid_swa_sink_hgrn2_macaron_ple"


def rmsnorm(x, w):
    x32 = x.astype(jnp.float32)
    y = x32 * lax.rsqrt(jnp.mean(x32 * x32, axis=-1, keepdims=True) + EPS)
    return (y * w.astype(jnp.float32)).astype(x.dtype)


def swiglu(h, w_gate, w_up, w_down):
    return (jax.nn.silu(h @ w_gate) * (h @ w_up)) @ w_down


def partial_rope(t, positions):
    t32 = t.astype(jnp.float32)
    rot, rest = t32[..., :ROT_DIM], t32[..., ROT_DIM:]
    inv_freq = jnp.power(jnp.float32(ROPE_THETA), -jnp.arange(0, ROT_DIM, 2, dtype=jnp.float32) / ROT_DIM)
    ang = positions.astype(jnp.float32)[..., None] * inv_freq
    cos, sin = jnp.cos(ang)[:, :, None, :], jnp.sin(ang)[:, :, None, :]
    x1, x2 = rot[..., :ROT_DIM // 2], rot[..., ROT_DIM // 2:]
    out = jnp.concatenate([x1 * cos - x2 * sin, x2 * cos + x1 * sin, rest], axis=-1)
    return out.astype(t.dtype)


def sliding_window_attention(q, k, v, sinks):
    b, s = q.shape[0], q.shape[1]
    nb = s // BLOCK
    g = A_HEADS // A_KV_HEADS
    qb = q.reshape(b, nb, BLOCK, A_KV_HEADS, g, A_HEAD_DIM)
    kb = k.reshape(b, nb, BLOCK, A_KV_HEADS, A_HEAD_DIM)
    vb = v.reshape(b, nb, BLOCK, A_KV_HEADS, A_HEAD_DIM)

    def with_prev(t):
        prev = jnp.pad(t[:, :-1], ((0, 0), (1, 0), (0, 0), (0, 0), (0, 0)))
        return jnp.concatenate([prev, t], axis=2)

    kk, vv = with_prev(kb), with_prev(vb)
    scale = A_HEAD_DIM ** -0.5
    scores = jnp.einsum('bnqhgd,bnkhd->bnhgqk', qb, kk).astype(jnp.float32) * scale
    qi = jnp.arange(BLOCK)[:, None]
    kj = jnp.arange(2 * BLOCK)[None, :]
    dist = qi + BLOCK - kj
    blk = jnp.arange(nb)[:, None, None]
    allowed = (dist >= 0) & (dist < WINDOW) & ((blk > 0) | (kj >= BLOCK))
    scores = jnp.where(allowed[None, :, None, None], scores, -jnp.inf)
    sink = sinks.astype(jnp.float32).reshape(A_KV_HEADS, g)[None, None, :, :, None, None]
    m = jnp.maximum(jnp.max(scores, axis=-1, keepdims=True), sink)
    e = jnp.exp(scores - m)
    denom = jnp.sum(e, axis=-1, keepdims=True) + jnp.exp(sink - m)
    probs = (e / denom).astype(v.dtype)
    o = jnp.einsum('bnhgqk,bnkhd->bnqhgd', probs, vv)
    return o.reshape(b, s, A_WIDTH)


def hgrn2_chunk_step(state, inp):
    q, k, v, log_f = inp
    cum = jnp.cumsum(log_f, axis=2)
    causal = jnp.tril(jnp.ones((CHUNK, CHUNK), dtype=bool))
    diff = cum[:, :, :, None, :] - cum[:, :, None, :, :]
    decay = jnp.exp(jnp.where(causal[None, None, :, :, None], diff, -jnp.inf))
    scores = jnp.einsum('bhtd,bhsd,bhtsd->bhts', q, k, decay)
    o = scores @ v + jnp.einsum('bhtd,bhde->bhte', q * jnp.exp(cum), state)
    last = cum[:, :, -1:, :]
    new_state = jnp.exp(last[:, :, 0, :])[..., None] * state + jnp.einsum('bhsd,bhse->bhde', k * jnp.exp(last - cum), v)
    return new_state, o


def hgrn2(q_pre, f_pre, i_in, lb):
    b, s = q_pre.shape[0], q_pre.shape[1]
    nc = s // CHUNK
    f32 = f_pre.astype(jnp.float32)
    q = jax.nn.silu(q_pre.astype(jnp.float32))
    log_f = jnp.logaddexp(jnp.log(lb), jnp.log1p(-lb) + jax.nn.log_sigmoid(f32))
    k = (1.0 - lb) * jax.nn.sigmoid(-f32)
    v = i_in.astype(jnp.float32)

    def to_chunks(t):
        return t.reshape(b, nc, CHUNK, B_HEADS, t.shape[-1]).transpose(1, 0, 3, 2, 4)

    state0 = jnp.zeros((b, B_HEADS, B_KEY_DIM, B_VAL_DIM), jnp.float32)
    _, o = lax.scan(hgrn2_chunk_step, state0, (to_chunks(q), to_chunks(k), to_chunks(v), to_chunks(log_f)))
    return o.transpose(1, 0, 3, 2, 4).reshape(b, s, B_HEADS, B_VAL_DIM)


def setup_inputs(seed: int = 0) -> dict:
    key = jax.random.key(seed)
    ks = jax.random.split(key, 24)

    def nrm(k, shape, fan_in):
        return jax.random.normal(k, shape, jnp.float32) * fan_in ** -0.5

    def gain(k, shape):
        return 1.0 + 0.02 * jax.random.normal(k, shape, jnp.float32)

    return {
        "x": jax.random.normal(ks[0], (BATCH, SEQ, D_MODEL), jnp.float32),
        "p": jax.random.normal(ks[1], (DEPTH, BATCH, SEQ, PLE_DIM), jnp.float32),
        "positions": jnp.broadcast_to(jnp.arange(SEQ, dtype=jnp.int32), (BATCH, SEQ)),
        "ffn1_norm": gain(ks[2], (DEPTH, D_MODEL)),
        "ffn1_w_gate": nrm(ks[3], (DEPTH, D_MODEL, FF_DIM), D_MODEL),
        "ffn1_w_up": nrm(ks[4], (DEPTH, D_MODEL, FF_DIM), D_MODEL),
        "ffn1_w_down": nrm(ks[5], (DEPTH, FF_DIM, D_MODEL), FF_DIM),
        "mix_norm": gain(ks[6], (DEPTH, D_MODEL)),
        "w_in": nrm(ks[7], (DEPTH, D_MODEL, IN_DIM), D_MODEL),
        "attn_sinks": 0.5 * jax.random.normal(ks[8], (DEPTH, A_HEADS), jnp.float32),
        "hgrn_lower_bound": 0.1 * jax.random.normal(ks[9], (DEPTH + 1, B_KEY_WIDTH), jnp.float32),
        "hgrn_norm": gain(ks[10], (DEPTH, B_WIDTH)),
        "w_up_a": nrm(ks[11], (DEPTH, A_WIDTH, D_MODEL), A_WIDTH),
        "w_up_b": nrm(ks[12], (DEPTH, B_WIDTH, D_MODEL), B_WIDTH),
        "w_out": nrm(ks[13], (DEPTH, D_MODEL, D_MODEL), D_MODEL),
        "ffn2_norm": gain(ks[14], (DEPTH, D_MODEL)),
        "ffn2_w_gate": nrm(ks[15], (DEPTH, D_MODEL, FF_DIM), D_MODEL),
        "ffn2_w_up": nrm(ks[16], (DEPTH, D_MODEL, FF_DIM), D_MODEL),
        "ffn2_w_down": nrm(ks[17], (DEPTH, FF_DIM, D_MODEL), FF_DIM),
        "ple_norm": gain(ks[18], (DEPTH, D_MODEL)),
        "ple_w_gate": nrm(ks[19], (DEPTH, D_MODEL, D_MODEL), D_MODEL),
        "ple_w_proj": nrm(ks[20], (DEPTH, PLE_DIM, D_MODEL), PLE_DIM),
        "final_norm": gain(ks[21], (D_MODEL,)),
    }


def reference(x, p, positions, ffn1_norm, ffn1_w_gate, ffn1_w_up, ffn1_w_down, mix_norm, w_in,
              attn_sinks, hgrn_lower_bound, hgrn_norm, w_up_a, w_up_b, w_out, ffn2_norm,
              ffn2_w_gate, ffn2_w_up, ffn2_w_down, ple_norm, ple_w_gate, ple_w_proj, final_norm):
    b, s = x.shape[0], x.shape[1]
    offsets = np.cumsum(SPLITS)[:-1].tolist()
    lb_all = jnp.cumsum(jax.nn.softmax(hgrn_lower_bound.astype(jnp.float32), axis=0), axis=0)
    for l in range(DEPTH):
        h = rmsnorm(x, ffn1_norm[l])
        x = x + 0.5 * swiglu(h, ffn1_w_gate[l], ffn1_w_up[l], ffn1_w_down[l])

        h = rmsnorm(x, mix_norm[l])
        proj = h @ w_in[l]
        q_a, k_a, v_a, q_b, f_b, i_b, og_b, gate_a, gate_b = jnp.split(proj, offsets, axis=-1)

        q_a = partial_rope(q_a.reshape(b, s, A_HEADS, A_HEAD_DIM), positions)
        k_a = partial_rope(k_a.reshape(b, s, A_KV_HEADS, A_HEAD_DIM), positions)
        v_a = v_a.reshape(b, s, A_KV_HEADS, A_HEAD_DIM)
        out_a = sliding_window_attention(q_a, k_a, v_a, attn_sinks[l])

        lb = lb_all[l].reshape(B_HEADS, B_KEY_DIM)
        o_b = hgrn2(q_b.reshape(b, s, B_HEADS, B_KEY_DIM), f_b.reshape(b, s, B_HEADS, B_KEY_DIM),
                    i_b.reshape(b, s, B_HEADS, B_VAL_DIM), lb)
        o_b = rmsnorm(o_b, hgrn_norm[l].reshape(B_HEADS, B_VAL_DIM)).astype(x.dtype)
        out_b = (o_b * jax.nn.silu(og_b.reshape(b, s, B_HEADS, B_VAL_DIM))).reshape(b, s, B_WIDTH)

        merged = jax.nn.sigmoid(gate_a) * (out_a @ w_up_a[l]) + jax.nn.sigmoid(gate_b) * (out_b @ w_up_b[l])
        x = x + merged @ w_out[l]

        h = rmsnorm(x, ffn2_norm[l])
        x = x + 0.5 * swiglu(h, ffn2_w_gate[l], ffn2_w_up[l], ffn2_w_down[l])

        g = jax.nn.sigmoid(rmsnorm(x, ple_norm[l]) @ ple_w_gate[l])
        x = x + g * (p[l].astype(x.dtype) @ ple_w_proj[l])
    return rmsnorm(x, final_norm)
```

```python
import functools

import jax
import jax.numpy as jnp
import numpy as np
from jax import lax
from jax.experimental import pallas as pl
from jax.experimental.pallas import tpu as pltpu

EPS = 1e-6
A_HEADS = 16
A_KV_HEADS = 4
A_HEAD_DIM = 64
A_WIDTH = A_HEADS * A_HEAD_DIM
A_KV_WIDTH = A_KV_HEADS * A_HEAD_DIM
WINDOW = 128
ROT_DIM = A_HEAD_DIM // 4
ROPE_THETA = 500000.0
B_HEADS = 8
B_DIM = 128
B_WIDTH = B_HEADS * B_DIM

LANES = 128
SUBLANES = 8
VMEM_LIMIT = 56 * 1024 * 1024

F32 = jnp.float32
BF16 = jnp.bfloat16
NT_DIMS = (((1,), (1,)), ((), ()))
TN_DIMS = (((0,), (0,)), ((), ()))


def _params(*sem):
    return pltpu.CompilerParams(dimension_semantics=sem, vmem_limit_bytes=VMEM_LIMIT)


def _rmsnorm(x, w):
    return x * lax.rsqrt(jnp.mean(x * x, axis=-1, keepdims=True) + EPS) * w


def _sigmoid(x):
    return 1.0 / (1.0 + jnp.exp(-x))


def _ffn_kernel(x_ref, nw_ref, wg_ref, wu_ref, wd_ref, o_ref, h_ref, acc_ref):
    f = pl.program_id(1)

    @pl.when(f == 0)
    def _():
        h_ref[...] = _rmsnorm(x_ref[...], nw_ref[...]).astype(BF16)
        acc_ref[...] = jnp.zeros_like(acc_ref)

    h = h_ref[...]
    g = jnp.dot(h, wg_ref[...], preferred_element_type=F32)
    u = jnp.dot(h, wu_ref[...], preferred_element_type=F32)
    a = (g * _sigmoid(g) * u).astype(BF16)
    acc_ref[...] += jnp.dot(a, wd_ref[...], preferred_element_type=F32)

    @pl.when(f == pl.num_programs(1) - 1)
    def _():
        o_ref[...] = x_ref[...] + 0.5 * acc_ref[...]


def _ffn(x, nw, wg, wu, wd, tm, tf):
    t, d = x.shape
    ff = wg.shape[1]
    return pl.pallas_call(
        _ffn_kernel,
        grid=(t // tm, ff // tf),
        in_specs=[
            pl.BlockSpec((tm, d), lambda i, f: (i, 0)),
            pl.BlockSpec((1, d), lambda i, f: (0, 0)),
            pl.BlockSpec((d, tf), lambda i, f: (0, f)),
            pl.BlockSpec((d, tf), lambda i, f: (0, f)),
            pl.BlockSpec((tf, d), lambda i, f: (f, 0)),
        ],
        out_specs=pl.BlockSpec((tm, d), lambda i, f: (i, 0)),
        out_shape=jax.ShapeDtypeStruct((t, d), F32),
        scratch_shapes=[pltpu.VMEM((tm, d), BF16), pltpu.VMEM((tm, d), F32)],
        compiler_params=_params("parallel", "arbitrary"),
        name="ffn",
    )(x, nw, wg, wu, wd)


def _inproj_kernel(x_ref, nw_ref, w_ref, o_ref, h_ref):
    @pl.when(pl.program_id(1) == 0)
    def _():
        h_ref[...] = _rmsnorm(x_ref[...], nw_ref[...]).astype(BF16)

    o_ref[...] = jnp.dot(h_ref[...], w_ref[...], preferred_element_type=F32).astype(o_ref.dtype)


def _inproj(x, nw, w, tm, tn):
    t, d = x.shape
    n = w.shape[1]
    return pl.pallas_call(
        _inproj_kernel,
        grid=(t // tm, n // tn),
        in_specs=[
            pl.BlockSpec((tm, d), lambda i, j: (i, 0)),
            pl.BlockSpec((1, d), lambda i, j: (0, 0)),
            pl.BlockSpec((d, tn), lambda i, j: (0, j)),
        ],
        out_specs=pl.BlockSpec((tm, tn), lambda i, j: (i, j)),
        out_shape=jax.ShapeDtypeStruct((t, n), BF16),
        scratch_shapes=[pltpu.VMEM((tm, d), BF16)],
        compiler_params=_params("parallel", "arbitrary"),
        name="inproj",
    )(x, nw, w)


def _attn_kernel(sink_ref, invf_ref, posc_ref, posp_ref, q_ref, kc_ref, vc_ref, kp_ref, vp_ref, o_ref):
    n = pl.program_id(1)
    blk = q_ref.shape[0]
    lane = lax.broadcasted_iota(jnp.int32, (1, LANES), 1)
    low = lane < A_HEAD_DIM
    d = lane % A_HEAD_DIM
    invf = invf_ref[...]

    def tables(pos_ref):
        ang = pos_ref[...].astype(F32) * invf
        cs, sn = jnp.cos(ang), jnp.sin(ang)
        return cs, jnp.where(d < ROT_DIM // 2, -sn, 0.0), jnp.where(d >= ROT_DIM // 2, sn, 0.0)

    def rope(x, tabs):
        w = x.shape[1]
        cs, s1, s2 = (jnp.concatenate([tb] * (w // LANES), axis=1) for tb in tabs)
        return x * cs + pltpu.roll(x, w - ROT_DIM // 2, 1) * s1 + pltpu.roll(x, ROT_DIM // 2, 1) * s2

    tc, tp = tables(posc_ref), tables(posp_ref)
    q = rope(q_ref[...].astype(F32), tc) * (A_HEAD_DIM ** -0.5)
    k = jnp.concatenate([rope(kp_ref[...].astype(F32), tp), rope(kc_ref[...].astype(F32), tc)], axis=0)
    v = jnp.concatenate([vp_ref[...], vc_ref[...]], axis=0).astype(F32)

    qi = lax.broadcasted_iota(jnp.int32, (blk, 2 * blk), 0)
    kj = lax.broadcasted_iota(jnp.int32, (blk, 2 * blk), 1)
    allowed = (kj > qi) & (kj <= qi + blk) & ((n > 0) | (kj >= blk))

    grp = A_HEADS // A_KV_HEADS
    for g in range(A_KV_HEADS):
        c0 = (g // 2) * LANES
        own = low if g % 2 == 0 else ~low
        kc, vc = k[:, c0:c0 + LANES], v[:, c0:c0 + LANES]
        k2 = jnp.where(own, kc, pltpu.roll(kc, A_HEAD_DIM, 1)).astype(BF16)
        v2 = jnp.where(own, vc, pltpu.roll(vc, A_HEAD_DIM, 1)).astype(BF16)
        qs = []
        for pr in range(grp // 2):
            qp = q[:, (g * grp + 2 * pr) * A_HEAD_DIM:(g * grp + 2 * pr) * A_HEAD_DIM + LANES]
            qs += [jnp.where(low, qp, 0.0), jnp.where(low, 0.0, qp)]
        qst = jnp.concatenate(qs, axis=0).astype(BF16)
        s = lax.dot_general(qst, k2, NT_DIMS, preferred_element_type=F32)
        es, rden = [], []
        for r in range(grp):
            sk = sink_ref[g * grp + r]
            sr = jnp.where(allowed, s[r * blk:(r + 1) * blk], -jnp.inf)
            m = jnp.maximum(jnp.max(sr, axis=-1, keepdims=True), sk)
            e = jnp.exp(sr - m)
            rden.append(1.0 / (jnp.sum(e, axis=-1, keepdims=True) + jnp.exp(sk - m)))
            es.append(e.astype(BF16))
        o = jnp.dot(jnp.concatenate(es, axis=0), v2, preferred_element_type=F32)
        for pr in range(grp // 2):
            oa = o[(2 * pr) * blk:(2 * pr + 1) * blk] * rden[2 * pr]
            ob = o[(2 * pr + 1) * blk:(2 * pr + 2) * blk] * rden[2 * pr + 1]
            col = (g * grp + 2 * pr) * A_HEAD_DIM
            o_ref[:, col:col + LANES] = jnp.where(low, oa, ob).astype(o_ref.dtype)


def _attention(proj, pos, sinks, invf, batch, seq):
    nb = seq // WINDOW
    kcol = A_WIDTH // A_KV_WIDTH
    cur = lambda b, n: b * nb + n
    prev = lambda b, n: b * nb + jnp.maximum(n - 1, 0)
    return pl.pallas_call(
        _attn_kernel,
        grid=(batch, nb),
        in_specs=[
            pl.BlockSpec(memory_space=pltpu.SMEM),
            pl.BlockSpec((1, LANES), lambda b, n: (0, 0)),
            pl.BlockSpec((WINDOW, 1), lambda b, n: (cur(b, n), 0)),
            pl.BlockSpec((WINDOW, 1), lambda b, n: (prev(b, n), 0)),
            pl.BlockSpec((WINDOW, A_WIDTH), lambda b, n: (cur(b, n), 0)),
            pl.BlockSpec((WINDOW, A_KV_WIDTH), lambda b, n: (cur(b, n), kcol)),
            pl.BlockSpec((WINDOW, A_KV_WIDTH), lambda b, n: (cur(b, n), kcol + 1)),
            pl.BlockSpec((WINDOW, A_KV_WIDTH), lambda b, n: (prev(b, n), kcol)),
            pl.BlockSpec((WINDOW, A_KV_WIDTH), lambda b, n: (prev(b, n), kcol + 1)),
        ],
        out_specs=pl.BlockSpec((WINDOW, A_WIDTH), lambda b, n: (cur(b, n), 0)),
        out_shape=jax.ShapeDtypeStruct((batch * seq, A_WIDTH), BF16),
        compiler_params=_params("parallel", "arbitrary"),
        name="swa",
    )(sinks, invf, pos, pos, proj, proj, proj, proj, proj)


HGRN_CHUNK = 256


def _hgrn_kernel(lbp_ref, nw_ref, q_ref, f_ref, i_ref, og_ref, o_ref, st_ref, c_s, k_s, v_s):
    cn = q_ref.shape[0]
    nblk = cn // SUBLANES

    @pl.when(pl.program_id(2) == 0)
    def _():
        st_ref[...] = jnp.zeros_like(st_ref)

    a = lbp_ref[...]
    ea = jnp.exp(a - jnp.max(a, axis=0, keepdims=True))
    lb = ea[0:1] / jnp.sum(ea, axis=0, keepdims=True)

    qp = q_ref[...].astype(F32)
    q = qp * _sigmoid(qp)
    fg = lb + (1.0 - lb) * _sigmoid(f_ref[...].astype(F32))
    k = 1.0 - fg
    v = i_ref[...].astype(F32)
    k_s[...] = k
    v_s[...] = v

    row8 = lax.broadcasted_iota(jnp.int32, (cn, LANES), 0) % SUBLANES
    w = jnp.log(fg)
    for sh in (1, 2, 4):
        w = w + jnp.where(row8 >= sh, pltpu.roll(w, sh, 0), 0.0)
    c_s[...] = w
    run = jnp.zeros((SUBLANES, LANES), F32)
    cb = []
    for m in range(nblk):
        cb.append(w[m * SUBLANES:(m + 1) * SUBLANES] + run)
        run = run + c_s[pl.ds(m * SUBLANES + SUBLANES - 1, 1), :]
    c = jnp.concatenate(cb, axis=0)
    c_s[...] = c
    qb = [q[m * SUBLANES:(m + 1) * SUBLANES] for m in range(nblk)]
    kb = [k[m * SUBLANES:(m + 1) * SUBLANES] for m in range(nblk)]
    zero = jnp.zeros((SUBLANES, LANES), F32)

    ri = lax.broadcasted_iota(jnp.int32, (cn, cn), 0)
    ci = lax.broadcasted_iota(jnp.int32, (cn, cn), 1)
    scores = jnp.zeros((cn, cn), F32)
    b = SUBLANES
    while b < cn:
        qt, kt = [], []
        for m in range(nblk):
            anchor = (m * SUBLANES) // (2 * b) * (2 * b) + b
            ca = c_s[pl.ds(anchor - 1, 1), :]
            if m * SUBLANES >= anchor:
                qt.append(qb[m] * jnp.exp(cb[m] - ca))
                kt.append(zero)
            else:
                qt.append(zero)
                kt.append(kb[m] * jnp.exp(ca - cb[m]))
        sb = lax.dot_general(jnp.concatenate(qt, axis=0).astype(BF16), jnp.concatenate(kt, axis=0).astype(BF16),
                             NT_DIMS, preferred_element_type=F32)
        if 2 * b < cn:
            sb = jnp.where(ri // (2 * b) == ci // (2 * b), sb, 0.0)
        scores = scores + sb
        b *= 2

    r8 = lax.broadcasted_iota(jnp.int32, (SUBLANES, LANES), 0)
    od = [zero] * nblk
    for j in range(SUBLANES):
        for m in range(nblk):
            row = pl.ds(m * SUBLANES + j, 1)
            dec = jnp.exp(jnp.where(r8 >= j, cb[m] - c_s[row, :], -jnp.inf))
            sj = jnp.sum(qb[m] * k_s[row, :] * dec, axis=-1, keepdims=True)
            od[m] = od[m] + sj * v_s[row, :]

    st = st_ref[...]
    vb = v.astype(BF16)
    o = (jnp.dot(scores.astype(BF16), vb, preferred_element_type=F32)
         + jnp.concatenate(od, axis=0)
         + lax.dot_general((q * jnp.exp(c)).astype(BF16), st.astype(BF16), NT_DIMS, preferred_element_type=F32))

    clast = c[cn - 1:cn]
    kl = (k * jnp.exp(clast - c)).astype(BF16)
    st_ref[...] = jnp.exp(clast) * st + lax.dot_general(vb, kl, TN_DIMS, preferred_element_type=F32)

    og = og_ref[...].astype(F32)
    o_ref[...] = (_rmsnorm(o, nw_ref[...]) * (og * _sigmoid(og))).astype(o_ref.dtype)


def _hgrn(proj, lbp, nw, batch, seq):
    cn = min(HGRN_CHUNK, seq)
    nc = seq // cn
    base = (A_WIDTH + 2 * A_KV_WIDTH) // B_DIM
    col = lambda part: (lambda b, h, t: (b * nc + t, base + part * B_HEADS + h))
    return pl.pallas_call(
        _hgrn_kernel,
        grid=(batch, B_HEADS, nc),
        in_specs=[
            pl.BlockSpec((lbp.shape[0], B_DIM), lambda b, h, t: (0, h)),
            pl.BlockSpec((1, B_DIM), lambda b, h, t: (0, h)),
            pl.BlockSpec((cn, B_DIM), col(0)),
            pl.BlockSpec((cn, B_DIM), col(1)),
            pl.BlockSpec((cn, B_DIM), col(2)),
            pl.BlockSpec((cn, B_DIM), col(3)),
        ],
        out_specs=pl.BlockSpec((cn, B_DIM), lambda b, h, t: (b * nc + t, h)),
        out_shape=jax.ShapeDtypeStruct((batch * seq, B_WIDTH), BF16),
        scratch_shapes=[pltpu.VMEM((B_DIM, B_DIM), F32), pltpu.VMEM((cn, B_DIM), F32),
                        pltpu.VMEM((cn, B_DIM), F32), pltpu.VMEM((cn, B_DIM), F32)],
        compiler_params=_params("parallel", "parallel", "arbitrary"),
        name="hgrn2",
    )(lbp, nw, proj, proj, proj, proj)


def _merge_kernel(x_ref, oa_ref, ob_ref, ga_ref, gb_ref, wa_ref, wb_ref, wo_ref, o_ref, acc_ref):
    j = pl.program_id(1)

    @pl.when(j == 0)
    def _():
        acc_ref[...] = jnp.zeros_like(acc_ref)

    ma = jnp.dot(oa_ref[...], wa_ref[...], preferred_element_type=F32)
    mb = jnp.dot(ob_ref[...], wb_ref[...], preferred_element_type=F32)
    merged = _sigmoid(ga_ref[...].astype(F32)) * ma + _sigmoid(gb_ref[...].astype(F32)) * mb
    acc_ref[...] += jnp.dot(merged.astype(BF16), wo_ref[...], preferred_element_type=F32)

    @pl.when(j == pl.num_programs(1) - 1)
    def _():
        o_ref[...] = x_ref[...] + acc_ref[...]


def _merge(x, out_a, out_b, proj, wa, wb, wo, tm, tn):
    t, d = x.shape
    ga0 = (proj.shape[1] - 2 * d) // tn
    gb0 = (proj.shape[1] - d) // tn
    return pl.pallas_call(
        _merge_kernel,
        grid=(t // tm, d // tn),
        in_specs=[
            pl.BlockSpec((tm, d), lambda i, j: (i, 0)),
            pl.BlockSpec((tm, A_WIDTH), lambda i, j: (i, 0)),
            pl.BlockSpec((tm, B_WIDTH), lambda i, j: (i, 0)),
            pl.BlockSpec((tm, tn), lambda i, j: (i, ga0 + j)),
            pl.BlockSpec((tm, tn), lambda i, j: (i, gb0 + j)),
            pl.BlockSpec((A_WIDTH, tn), lambda i, j: (0, j)),
            pl.BlockSpec((B_WIDTH, tn), lambda i, j: (0, j)),
            pl.BlockSpec((tn, d), lambda i, j: (j, 0)),
        ],
        out_specs=pl.BlockSpec((tm, d), lambda i, j: (i, 0)),
        out_shape=jax.ShapeDtypeStruct((t, d), F32),
        scratch_shapes=[pltpu.VMEM((tm, d), F32)],
        compiler_params=_params("parallel", "arbitrary"),
        name="merge",
    )(x, out_a, out_b, proj, proj, wa, wb, wo)


def _ple_kernel(x_ref, p_ref, nw_ref, fw_ref, wg_ref, wp_ref, o_ref):
    x = x_ref[...]
    h = _rmsnorm(x, nw_ref[...]).astype(BF16)
    g = _sigmoid(jnp.dot(h, wg_ref[...], preferred_element_type=F32))
    e = jnp.dot(p_ref[...].astype(BF16), wp_ref[...], preferred_element_type=F32)
    o_ref[...] = _rmsnorm(x + g * e, fw_ref[...])


def _ple(x, p, nw, fw, wg, wp, tm):
    t, d = x.shape
    pd = p.shape[1]
    return pl.pallas_call(
        _ple_kernel,
        grid=(t // tm,),
        in_specs=[
            pl.BlockSpec((tm, d), lambda i: (i, 0)),
            pl.BlockSpec((tm, pd), lambda i: (i, 0)),
            pl.BlockSpec((1, d), lambda i: (0, 0)),
            pl.BlockSpec((1, d), lambda i: (0, 0)),
            pl.BlockSpec((d, d), lambda i: (0, 0)),
            pl.BlockSpec((pd, d), lambda i: (0, 0)),
        ],
        out_specs=pl.BlockSpec((tm, d), lambda i: (i, 0)),
        out_shape=jax.ShapeDtypeStruct((t, d), F32),
        compiler_params=_params("parallel"),
        name="ple",
    )(x, p, nw, fw, wg, wp)


def _tiles(t, d, ff):
    return dict(ffn_tm=min(512, t), ffn_tf=min(512, ff), proj_tm=min(1024, t), proj_tn=512,
                merge_tm=min(512, t), merge_tn=min(512, d), ple_tm=min(256, t))


def kernel(x, p, positions, ffn1_norm, ffn1_w_gate, ffn1_w_up, ffn1_w_down, mix_norm, w_in, attn_sinks, hgrn_lower_bound, hgrn_norm, w_up_a, w_up_b, w_out, ffn2_norm, ffn2_w_gate, ffn2_w_up, ffn2_w_down, ple_norm, ple_w_gate, ple_w_proj, final_norm):
    batch, seq, d = x.shape
    assert ffn1_norm.shape[0] == 1, "single-layer stack"
    t = batch * seq
    ts = _tiles(t, d, ffn1_w_gate.shape[-1])
    bf = lambda wt: wt.astype(BF16)
    row = lambda wt: wt.reshape(1, -1).astype(F32)

    lane_d = np.arange(LANES) % A_HEAD_DIM
    inv_freq = jnp.power(jnp.float32(ROPE_THETA), -jnp.arange(0, ROT_DIM, 2, dtype=F32) / ROT_DIM)
    invf = jnp.where(lane_d < ROT_DIM, inv_freq[lane_d % (ROT_DIM // 2)], 0.0).reshape(1, LANES).astype(F32)

    x0 = x.reshape(t, d)
    x1 = _ffn(x0, row(ffn1_norm[0]), bf(ffn1_w_gate[0]), bf(ffn1_w_up[0]), bf(ffn1_w_down[0]),
              ts["ffn_tm"], ts["ffn_tf"])
    proj = _inproj(x1, row(mix_norm[0]), bf(w_in[0]), ts["proj_tm"], ts["proj_tn"])
    out_a = _attention(proj, positions.reshape(t, 1), attn_sinks[0].astype(F32), invf, batch, seq)
    out_b = _hgrn(proj, hgrn_lower_bound.astype(F32), row(hgrn_norm[0]), batch, seq)
    x2 = _merge(x1, out_a, out_b, proj, bf(w_up_a[0]), bf(w_up_b[0]), bf(w_out[0]),
                ts["merge_tm"], ts["merge_tn"])
    x3 = _ffn(x2, row(ffn2_norm[0]), bf(ffn2_w_gate[0]), bf(ffn2_w_up[0]), bf(ffn2_w_down[0]),
              ts["ffn_tm"], ts["ffn_tf"])
    out = _ple(x3, p[0].reshape(t, -1), row(ple_norm[0]), row(final_norm), bf(ple_w_gate[0]), bf(ple_w_proj[0]),
               ts["ple_tm"])
    return out.reshape(batch, seq, d)
```

```python
import functools

import jax
import jax.numpy as jnp
import numpy as np
from jax import lax
from jax.experimental import pallas as pl
from jax.experimental.pallas import tpu as pltpu

EPS = 1e-6
A_HEADS = 16
A_KV_HEADS = 4
A_HEAD_DIM = 64
A_WIDTH = A_HEADS * A_HEAD_DIM
A_KV_WIDTH = A_KV_HEADS * A_HEAD_DIM
WINDOW = 128
ROT_DIM = A_HEAD_DIM // 4
ROT_HALF = ROT_DIM // 2
ROPE_THETA = 500000.0
B_HEADS = 8
B_DIM = 128
B_WIDTH = B_HEADS * B_DIM

LANES = 128
SUBLANES = 8
VMEM_LIMIT = 56 * 1024 * 1024
LOG2E = 1.4426950408889634

F32 = jnp.float32
BF16 = jnp.bfloat16
NT_DIMS = (((1,), (1,)), ((), ()))
TN_DIMS = (((0,), (0,)), ((), ()))


def _params(*sem):
    return pltpu.CompilerParams(dimension_semantics=sem, vmem_limit_bytes=VMEM_LIMIT)


def _resident(shape):
    return pl.BlockSpec(shape, lambda *_: (0,) * len(shape), pipeline_mode=pl.Buffered(1))


def _rmsnorm(x, w):
    return x * lax.rsqrt(jnp.mean(x * x, axis=-1, keepdims=True) + EPS) * w


def _sigmoid(x):
    return 1.0 / (1.0 + jnp.exp(-x))


def _ffn_kernel(x_ref, nw_ref, wg_ref, wu_ref, wd_ref, o_ref, h_ref):
    @pl.when(pl.program_id(1) == 0)
    def _():
        x = x_ref[...]
        h_ref[...] = _rmsnorm(x, nw_ref[...]).astype(BF16)
        o_ref[...] = x

    h = h_ref[...]
    g = jnp.dot(h, wg_ref[...], preferred_element_type=F32)
    u = jnp.dot(h, wu_ref[...], preferred_element_type=F32)
    a = (0.5 * g * _sigmoid(g) * u).astype(BF16)
    o_ref[...] += jnp.dot(a, wd_ref[...], preferred_element_type=F32)


def _ffn(x, nw, wg, wu, wd, tm, tf):
    t, d = x.shape
    ff = wg.shape[1]
    return pl.pallas_call(
        _ffn_kernel,
        grid=(t // tm, ff // tf),
        in_specs=[
            pl.BlockSpec((tm, d), lambda i, f: (i, 0)),
            pl.BlockSpec((1, d), lambda i, f: (0, 0)),
            pl.BlockSpec((d, tf), lambda i, f: (0, f)),
            pl.BlockSpec((d, tf), lambda i, f: (0, f)),
            pl.BlockSpec((tf, d), lambda i, f: (f, 0)),
        ],
        out_specs=pl.BlockSpec((tm, d), lambda i, f: (i, 0)),
        out_shape=jax.ShapeDtypeStruct((t, d), F32),
        scratch_shapes=[pltpu.VMEM((tm, d), BF16)],
        compiler_params=_params("parallel", "arbitrary"),
        name="ffn",
    )(x, nw, wg, wu, wd)


def _rope_matrix():
    m = np.zeros((2 * ROT_HALF, 3 * LANES), np.float32)
    for lane in range(LANES):
        dd = lane % A_HEAD_DIM
        if dd < ROT_DIM:
            m[dd % ROT_HALF, lane] = 1.0
        if dd < ROT_HALF:
            m[ROT_HALF + dd, LANES + lane] = -1.0
        elif dd < ROT_DIM:
            m[ROT_HALF + dd - ROT_HALF, 2 * LANES + lane] = 1.0
    return np.concatenate([m, m], axis=0)


def _inproj_kernel(pos_ref, invf_ref, m2_ref, x_ref, nw_ref, w_ref, o_ref, h_ref, tab_ref, *, plan):
    j = pl.program_id(1)
    tm = x_ref.shape[0]

    @pl.when(j == 0)
    def _():
        h_ref[...] = _rmsnorm(x_ref[...], nw_ref[...]).astype(BF16)
        for gi in range(tm // LANES):
            ang = invf_ref[...] * pos_ref[gi].astype(F32)
            cs = jnp.concatenate([jnp.cos(ang) - 1.0, jnp.sin(ang)], axis=0)
            hi = cs.astype(BF16)
            lo = (cs - hi.astype(F32)).astype(BF16)
            tab_ref[gi * LANES:(gi + 1) * LANES, :] = lax.dot_general(
                jnp.concatenate([hi, lo], axis=0), m2_ref[...], TN_DIMS, preferred_element_type=F32)

    r = jnp.dot(h_ref[...], w_ref[...], preferred_element_type=F32)

    def emit(kinds):
        def body():
            cs = tab_ref[:, 0:LANES] + 1.0
            s1 = tab_ref[:, LANES:2 * LANES]
            s2 = tab_ref[:, 2 * LANES:3 * LANES]
            for gi, kind in enumerate(kinds):
                rg = r[:, gi * LANES:(gi + 1) * LANES]
                if kind is not None:
                    rg = rg * cs + pltpu.roll(rg, LANES - ROT_HALF, 1) * s1 + pltpu.roll(rg, ROT_HALF, 1) * s2
                if kind == "q":
                    rg = rg * (A_HEAD_DIM ** -0.5)
                o_ref[:, gi * LANES:(gi + 1) * LANES] = rg.astype(o_ref.dtype)
        return body

    for jj, kinds in enumerate(plan):
        pl.when(j == jj)(emit(kinds))

    @pl.when(j >= len(plan))
    def _():
        o_ref[...] = r.astype(o_ref.dtype)


def _inproj(x, nw, w, pos3, invf8, m2, tm, tn):
    t, d = x.shape
    n = w.shape[1]
    kind = lambda col: "q" if col < A_WIDTH else ("k" if col < A_WIDTH + A_KV_WIDTH else None)
    plan = [tuple(kind(jj * tn + gi * LANES) for gi in range(tn // LANES)) for jj in range(n // tn)]
    plan = plan[:max(jj + 1 for jj, kinds in enumerate(plan) if any(kinds))]
    return pl.pallas_call(
        functools.partial(_inproj_kernel, plan=plan),
        grid=(t // tm, n // tn),
        in_specs=[
            pl.BlockSpec((tm // LANES, 1, LANES), lambda i, j: (i, 0, 0)),
            pl.BlockSpec(invf8.shape, lambda i, j: (0, 0)),
            pl.BlockSpec(m2.shape, lambda i, j: (0, 0)),
            pl.BlockSpec((tm, d), lambda i, j: (i, 0)),
            pl.BlockSpec((1, d), lambda i, j: (0, 0)),
            pl.BlockSpec((d, tn), lambda i, j: (0, j)),
        ],
        out_specs=pl.BlockSpec((tm, tn), lambda i, j: (i, j)),
        out_shape=jax.ShapeDtypeStruct((t, n), BF16),
        scratch_shapes=[pltpu.VMEM((tm, d), BF16), pltpu.VMEM((tm, 3 * LANES), F32)],
        compiler_params=_params("parallel", "arbitrary"),
        name="inproj",
    )(pos3, invf8, m2, x, nw, w)


def _attn_kernel(sink_ref, q_ref, k_ref, v_ref, o_ref, k_s, v_s):
    n = pl.program_id(1)
    blk = q_ref.shape[0]
    low = lax.broadcasted_iota(jnp.int32, (1, LANES), 1) < A_HEAD_DIM

    @pl.when(n == 0)
    def _():
        k_s[...] = jnp.zeros_like(k_s)
        v_s[...] = jnp.zeros_like(v_s)

    qi = lax.broadcasted_iota(jnp.int32, (blk, 2 * blk), 0)
    kj = lax.broadcasted_iota(jnp.int32, (blk, 2 * blk), 1)
    allowed = (kj > qi) & (kj <= qi + blk) & ((n > 0) | (kj >= blk))

    grp = A_HEADS // A_KV_HEADS
    for g in range(A_KV_HEADS):
        c0 = (g // 2) * LANES
        own = low if g % 2 == 0 else jnp.logical_not(low)
        kc, vc = k_ref[:, c0:c0 + LANES].astype(F32), v_ref[:, c0:c0 + LANES].astype(F32)
        k2 = jnp.where(own, kc, pltpu.roll(kc, A_HEAD_DIM, 1)).astype(BF16)
        v2 = jnp.where(own, vc, pltpu.roll(vc, A_HEAD_DIM, 1)).astype(BF16)
        k_s[g, blk:, :] = k2
        v_s[g, blk:, :] = v2
        qs = []
        for pr in range(grp // 2):
            col = (g * grp + 2 * pr) * A_HEAD_DIM
            qp = q_ref[:, col:col + LANES].astype(F32)
            qs += [jnp.where(low, qp, 0.0), jnp.where(low, 0.0, qp)]
        qst = jnp.concatenate(qs, axis=0).astype(BF16)
        s = lax.dot_general(qst, k_s[g], NT_DIMS, preferred_element_type=F32)
        es, rden = [], []
        for r in range(grp):
            sk = sink_ref[g * grp + r]
            sr = jnp.where(allowed, s[r * blk:(r + 1) * blk], -jnp.inf)
            m = jnp.maximum(jnp.max(sr, axis=-1, keepdims=True), sk)
            e = jnp.exp(sr - m)
            rden.append(1.0 / (jnp.sum(e, axis=-1, keepdims=True) + jnp.exp(sk - m)))
            es.append(e.astype(BF16))
        o = jnp.dot(jnp.concatenate(es, axis=0), v_s[g], preferred_element_type=F32)
        for pr in range(grp // 2):
            oa = o[(2 * pr) * blk:(2 * pr + 1) * blk] * rden[2 * pr]
            ob = o[(2 * pr + 1) * blk:(2 * pr + 2) * blk] * rden[2 * pr + 1]
            col = (g * grp + 2 * pr) * A_HEAD_DIM
            o_ref[:, col:col + LANES] = jnp.where(low, oa, ob).astype(o_ref.dtype)
        k_s[g, :blk, :] = k2
        v_s[g, :blk, :] = v2


def _attention(proj, sinks, batch, seq):
    nb = seq // WINDOW
    kcol = A_WIDTH // A_KV_WIDTH
    cur = lambda b, n: b * nb + n
    return pl.pallas_call(
        _attn_kernel,
        grid=(batch, nb),
        in_specs=[
            pl.BlockSpec(memory_space=pltpu.SMEM),
            pl.BlockSpec((WINDOW, A_WIDTH), lambda b, n: (cur(b, n), 0)),
            pl.BlockSpec((WINDOW, A_KV_WIDTH), lambda b, n: (cur(b, n), kcol)),
            pl.BlockSpec((WINDOW, A_KV_WIDTH), lambda b, n: (cur(b, n), kcol + 1)),
        ],
        out_specs=pl.BlockSpec((WINDOW, A_WIDTH), lambda b, n: (cur(b, n), 0)),
        out_shape=jax.ShapeDtypeStruct((batch * seq, A_WIDTH), BF16),
        scratch_shapes=[pltpu.VMEM((A_KV_HEADS, 2 * WINDOW, LANES), BF16),
                        pltpu.VMEM((A_KV_HEADS, 2 * WINDOW, LANES), BF16)],
        compiler_params=_params("parallel", "arbitrary"),
        name="swa",
    )(sinks, proj, proj, proj)


HGRN_CHUNK = 256
HGRN_QUAD = 128


def _hgrn_kernel(lbp_ref, nw_ref, q_ref, f_ref, i_ref, og_ref, o_ref, st_ref, c_s, kc_s, v_s):
    cn = q_ref.shape[0]
    nblk = cn // SUBLANES
    quad = min(HGRN_QUAD, cn)
    bpq = quad // SUBLANES
    assert cn in (quad, 2 * quad)

    @pl.when(pl.program_id(2) == 0)
    def _():
        st_ref[...] = jnp.zeros_like(st_ref)

    a = lbp_ref[...]
    ea = jnp.exp(a - jnp.max(a, axis=0, keepdims=True))
    lb = ea[0:1] / jnp.sum(ea, axis=0, keepdims=True)

    qp = q_ref[...].astype(F32)
    q = qp * _sigmoid(qp)
    fg = lb + (1.0 - lb) * _sigmoid(f_ref[...].astype(F32))
    v = i_ref[...].astype(F32)
    v_s[...] = v

    row8 = lax.broadcasted_iota(jnp.int32, (cn, LANES), 0) % SUBLANES
    w = jnp.log(fg) * LOG2E
    for sh in (1, 2, 4):
        w = w + jnp.where(row8 >= sh, pltpu.roll(w, sh, 0), 0.0)
    c_s[...] = w
    run = jnp.zeros((SUBLANES, LANES), F32)
    cb = []
    for m in range(nblk):
        cb.append(w[m * SUBLANES:(m + 1) * SUBLANES] + run)
        run = run + c_s[pl.ds(m * SUBLANES + SUBLANES - 1, 1), :]
    c = jnp.concatenate(cb, axis=0)
    c_s[...] = c
    kc = c - jnp.log(1.0 - fg) * LOG2E
    kc_s[...] = kc
    qb = [q[m * SUBLANES:(m + 1) * SUBLANES] for m in range(nblk)]
    kcb = [kc[m * SUBLANES:(m + 1) * SUBLANES] for m in range(nblk)]
    zero = jnp.zeros((SUBLANES, LANES), F32)

    def level(blocks_q, blocks_k, b):
        qt, kt = [], []
        for m in blocks_q:
            anchor = (m * SUBLANES) // (2 * b) * (2 * b) + b
            if m * SUBLANES >= anchor:
                qt.append(qb[m] * jnp.exp2(cb[m] - c_s[pl.ds(anchor - 1, 1), :]))
            else:
                qt.append(zero)
        for m in blocks_k:
            anchor = (m * SUBLANES) // (2 * b) * (2 * b) + b
            if m * SUBLANES < anchor:
                kt.append(jnp.exp2(c_s[pl.ds(anchor - 1, 1), :] - kcb[m]))
            else:
                kt.append(zero)
        return lax.dot_general(jnp.concatenate(qt, axis=0).astype(BF16), jnp.concatenate(kt, axis=0).astype(BF16),
                               NT_DIMS, preferred_element_type=F32)

    blkxor = (lax.broadcasted_iota(jnp.int32, (quad, quad), 0) ^ lax.broadcasted_iota(jnp.int32, (quad, quad), 1))
    sq = []
    for qd in range(cn // quad):
        blocks = list(range(qd * bpq, (qd + 1) * bpq))
        bs = []
        b = SUBLANES
        while b < quad:
            bs.append(b)
            b *= 2
        sc = level(blocks, blocks, bs[-1])
        for b in reversed(bs[:-1]):
            sc = jnp.where(blkxor < 2 * b, level(blocks, blocks, b), sc)
        sq.append(sc.astype(BF16))

    vb = v.astype(BF16)
    if cn == quad:
        o = jnp.dot(sq[0], vb, preferred_element_type=F32)
    else:
        lo_blocks, hi_blocks = list(range(bpq)), list(range(bpq, nblk))
        cross = level(hi_blocks, lo_blocks, quad).astype(BF16)
        o = jnp.concatenate([
            jnp.dot(sq[0], vb[:quad], preferred_element_type=F32),
            jnp.dot(jnp.concatenate([cross, sq[1]], axis=1), vb, preferred_element_type=F32)], axis=0)

    r8 = lax.broadcasted_iota(jnp.int32, (SUBLANES, LANES), 0)
    od = [zero] * nblk
    for j in range(SUBLANES):
        for m in range(nblk):
            row = pl.ds(m * SUBLANES + j, 1)
            dec = jnp.exp2(jnp.where(r8 >= j, cb[m] - kc_s[row, :], -jnp.inf))
            od[m] = od[m] + jnp.sum(qb[m] * dec, axis=-1, keepdims=True) * v_s[row, :]

    st = st_ref[...]
    o = (o + jnp.concatenate(od, axis=0)
         + lax.dot_general((q * jnp.exp2(c)).astype(BF16), st.astype(BF16), NT_DIMS, preferred_element_type=F32))

    clast = c[cn - 1:cn]
    kl = jnp.exp2(clast - kc).astype(BF16)
    st_ref[...] = jnp.exp2(clast) * st + lax.dot_general(vb, kl, TN_DIMS, preferred_element_type=F32)

    og = og_ref[...].astype(F32)
    o_ref[...] = (_rmsnorm(o, nw_ref[...]) * (og * _sigmoid(og))).astype(o_ref.dtype)


def _hgrn(proj, lbp, nw, batch, seq):
    cn = min(HGRN_CHUNK, seq)
    nc = seq // cn
    base = (A_WIDTH + 2 * A_KV_WIDTH) // B_DIM
    col = lambda part: (lambda b, h, t: (b * nc + t, base + part * B_HEADS + h))
    return pl.pallas_call(
        _hgrn_kernel,
        grid=(batch, B_HEADS, nc),
        in_specs=[
            pl.BlockSpec((lbp.shape[0], B_DIM), lambda b, h, t: (0, h)),
            pl.BlockSpec((1, B_DIM), lambda b, h, t: (0, h)),
            pl.BlockSpec((cn, B_DIM), col(0)),
            pl.BlockSpec((cn, B_DIM), col(1)),
            pl.BlockSpec((cn, B_DIM), col(2)),
            pl.BlockSpec((cn, B_DIM), col(3)),
        ],
        out_specs=pl.BlockSpec((cn, B_DIM), lambda b, h, t: (b * nc + t, h)),
        out_shape=jax.ShapeDtypeStruct((batch * seq, B_WIDTH), BF16),
        scratch_shapes=[pltpu.VMEM((B_DIM, B_DIM), F32), pltpu.VMEM((cn, B_DIM), F32),
                        pltpu.VMEM((cn, B_DIM), F32), pltpu.VMEM((cn, B_DIM), F32)],
        compiler_params=_params("parallel", "parallel", "arbitrary"),
        name="hgrn2",
    )(lbp, nw, proj, proj, proj, proj)


def _merge_kernel(*refs, nchunk):
    x_ref, oa_ref, ob_ref = refs[:3]
    ga_refs, gb_refs = refs[3:3 + nchunk], refs[3 + nchunk:3 + 2 * nchunk]
    wa_ref, wb_ref, wo_ref, o_ref = refs[3 + 2 * nchunk:]
    tn = ga_refs[0].shape[1]
    o_ref[...] = x_ref[...]
    oa, ob = oa_ref[...], ob_ref[...]
    for c in range(nchunk):
        sl = slice(c * tn, (c + 1) * tn)
        ma = jnp.dot(oa, wa_ref[:, sl], preferred_element_type=F32)
        mb = jnp.dot(ob, wb_ref[:, sl], preferred_element_type=F32)
        merged = _sigmoid(ga_refs[c][...].astype(F32)) * ma + _sigmoid(gb_refs[c][...].astype(F32)) * mb
        o_ref[...] += jnp.dot(merged.astype(BF16), wo_ref[sl, :], preferred_element_type=F32)


def _merge(x, out_a, out_b, proj, wa, wb, wo, tm, tn):
    t, d = x.shape
    nchunk = d // tn
    ga0 = (proj.shape[1] - 2 * d) // tn
    gb0 = (proj.shape[1] - d) // tn
    gate = lambda c0: [pl.BlockSpec((tm, tn), functools.partial(lambda i, cc: (i, cc), cc=c0 + c))
                       for c in range(nchunk)]
    return pl.pallas_call(
        functools.partial(_merge_kernel, nchunk=nchunk),
        grid=(t // tm,),
        in_specs=[
            pl.BlockSpec((tm, d), lambda i: (i, 0)),
            pl.BlockSpec((tm, A_WIDTH), lambda i: (i, 0)),
            pl.BlockSpec((tm, B_WIDTH), lambda i: (i, 0)),
            *gate(ga0), *gate(gb0),
            _resident(wa.shape), _resident(wb.shape), _resident(wo.shape),
        ],
        out_specs=pl.BlockSpec((tm, d), lambda i: (i, 0)),
        out_shape=jax.ShapeDtypeStruct((t, d), F32),
        compiler_params=_params("parallel"),
        name="merge",
    )(x, out_a, out_b, *([proj] * (2 * nchunk)), wa, wb, wo)


def _ple_kernel(x_ref, p_ref, nw_ref, fw_ref, wg_ref, wp_ref, o_ref, *, tn):
    x = x_ref[...]
    d = x.shape[1]
    h = _rmsnorm(x, nw_ref[...]).astype(BF16)
    pb = p_ref[...].astype(BF16)
    ss = jnp.zeros((x.shape[0], 1), F32)
    for c in range(d // tn):
        sl = slice(c * tn, (c + 1) * tn)
        g = _sigmoid(jnp.dot(h, wg_ref[:, sl], preferred_element_type=F32))
        y = x[:, sl] + g * jnp.dot(pb, wp_ref[:, sl], preferred_element_type=F32)
        ss = ss + jnp.sum(y * y, axis=-1, keepdims=True)
        o_ref[:, sl] = y
    o_ref[...] = o_ref[...] * lax.rsqrt(ss * (1.0 / d) + EPS) * fw_ref[...]


def _ple(x, p, nw, fw, wg, wp, tm, tn):
    t, d = x.shape
    pd = p.shape[1]
    return pl.pallas_call(
        functools.partial(_ple_kernel, tn=tn),
        grid=(t // tm,),
        in_specs=[
            pl.BlockSpec((tm, d), lambda i: (i, 0)),
            pl.BlockSpec((tm, pd), lambda i: (i, 0)),
            pl.BlockSpec((1, d), lambda i: (0, 0)),
            pl.BlockSpec((1, d), lambda i: (0, 0)),
            _resident(wg.shape), _resident(wp.shape),
        ],
        out_specs=pl.BlockSpec((tm, d), lambda i: (i, 0)),
        out_shape=jax.ShapeDtypeStruct((t, d), F32),
        compiler_params=_params("parallel"),
        name="ple",
    )(x, p, nw, fw, wg, wp)


def _tiles(t, d, ff):
    return dict(ffn_tm=min(1024, t), ffn_tf=min(512, ff), proj_tm=min(1024, t), proj_tn=512,
                merge_tm=min(512, t), merge_tn=min(512, d), ple_tm=min(512, t), ple_tn=min(512, d))


def kernel(x, p, positions, ffn1_norm, ffn1_w_gate, ffn1_w_up, ffn1_w_down, mix_norm, w_in, attn_sinks, hgrn_lower_bound, hgrn_norm, w_up_a, w_up_b, w_out, ffn2_norm, ffn2_w_gate, ffn2_w_up, ffn2_w_down, ple_norm, ple_w_gate, ple_w_proj, final_norm):
    batch, seq, d = x.shape
    assert ffn1_norm.shape[0] == 1, "single-layer stack"
    t = batch * seq
    ts = _tiles(t, d, ffn1_w_gate.shape[-1])
    bf = lambda wt: wt.astype(BF16)
    row = lambda wt: wt.reshape(1, -1).astype(F32)

    inv_freq = jnp.power(jnp.float32(ROPE_THETA), -jnp.arange(0, ROT_DIM, 2, dtype=F32) / ROT_DIM)
    invf8 = jnp.broadcast_to(inv_freq[:, None], (ROT_HALF, LANES)).astype(F32)
    m2 = jnp.asarray(_rope_matrix(), BF16)
    pos3 = positions.reshape(t // LANES, 1, LANES)

    x0 = x.reshape(t, d)
    x1 = _ffn(x0, row(ffn1_norm[0]), bf(ffn1_w_gate[0]), bf(ffn1_w_up[0]), bf(ffn1_w_down[0]),
              ts["ffn_tm"], ts["ffn_tf"])
    proj = _inproj(x1, row(mix_norm[0]), bf(w_in[0]), pos3, invf8, m2, ts["proj_tm"], ts["proj_tn"])
    out_a = _attention(proj, attn_sinks[0].astype(F32), batch, seq)
    out_b = _hgrn(proj, hgrn_lower_bound.astype(F32), row(hgrn_norm[0]), batch, seq)
    x2 = _merge(x1, out_a, out_b, proj, bf(w_up_a[0]), bf(w_up_b[0]), bf(w_out[0]),
                ts["merge_tm"], ts["merge_tn"])
    x3 = _ffn(x2, row(ffn2_norm[0]), bf(ffn2_w_gate[0]), bf(ffn2_w_up[0]), bf(ffn2_w_down[0]),
              ts["ffn_tm"], ts["ffn_tf"])
    out = _ple(x3, p[0].reshape(t, -1), row(ple_norm[0]), row(final_norm), bf(ple_w_gate[0]), bf(ple_w_proj[0]),
               ts["ple_tm"], ts["ple_tn"])
    return out.reshape(batch, seq, d)
```

```python
import functools

import jax
import jax.numpy as jnp
import numpy as np
from jax import lax
from jax.experimental import pallas as pl
from jax.experimental.pallas import tpu as pltpu

EPS = 1e-6
A_HEADS = 16
A_KV_HEADS = 4
A_HEAD_DIM = 64
A_WIDTH = A_HEADS * A_HEAD_DIM
A_KV_WIDTH = A_KV_HEADS * A_HEAD_DIM
WINDOW = 128
ROT_DIM = A_HEAD_DIM // 4
ROT_HALF = ROT_DIM // 2
ROPE_THETA = 500000.0
B_HEADS = 8
B_DIM = 128
B_WIDTH = B_HEADS * B_DIM

LANES = 128
SUBLANES = 8
VMEM_LIMIT = 56 * 1024 * 1024
LOG2E = 1.4426950408889634
Q_SCALE = A_HEAD_DIM ** -0.5 * LOG2E

F32 = jnp.float32
BF16 = jnp.bfloat16
NT_DIMS = (((1,), (1,)), ((), ()))
TN_DIMS = (((0,), (0,)), ((), ()))


def _params(*sem):
    return pltpu.CompilerParams(dimension_semantics=sem, vmem_limit_bytes=VMEM_LIMIT)


def _resident(shape):
    return pl.BlockSpec(shape, lambda *_: (0,) * len(shape), pipeline_mode=pl.Buffered(1))


def _rmsnorm(x, w):
    return x * lax.rsqrt(jnp.mean(x * x, axis=-1, keepdims=True) + EPS) * w


def _sigmoid(x):
    return 1.0 / (1.0 + jnp.exp(-x))


def _ffn_kernel(x_ref, nw_ref, wg_ref, wu_ref, wd_ref, o_ref, h_ref):
    @pl.when(pl.program_id(1) == 0)
    def _():
        x = x_ref[...]
        h_ref[...] = _rmsnorm(x, nw_ref[...]).astype(BF16)
        o_ref[...] = x

    h = h_ref[...]
    g = jnp.dot(h, wg_ref[...], preferred_element_type=F32)
    u = jnp.dot(h, wu_ref[...], preferred_element_type=F32)
    a = (0.5 * g * _sigmoid(g) * u).astype(BF16)
    o_ref[...] += jnp.dot(a, wd_ref[...], preferred_element_type=F32)


def _ffn(x, nw, wg, wu, wd, tm, tf):
    t, d = x.shape
    ff = wg.shape[1]
    return pl.pallas_call(
        _ffn_kernel,
        grid=(t // tm, ff // tf),
        in_specs=[
            pl.BlockSpec((tm, d), lambda i, f: (i, 0)),
            pl.BlockSpec((1, d), lambda i, f: (0, 0)),
            pl.BlockSpec((d, tf), lambda i, f: (0, f)),
            pl.BlockSpec((d, tf), lambda i, f: (0, f)),
            pl.BlockSpec((tf, d), lambda i, f: (f, 0)),
        ],
        out_specs=pl.BlockSpec((tm, d), lambda i, f: (i, 0)),
        out_shape=jax.ShapeDtypeStruct((t, d), F32),
        scratch_shapes=[pltpu.VMEM((tm, d), BF16)],
        compiler_params=_params("parallel", "arbitrary"),
        name="ffn",
    )(x, nw, wg, wu, wd)


def _rope_matrix():
    m = np.zeros((2 * ROT_HALF, 3 * LANES), np.float32)
    for lane in range(LANES):
        dd = lane % A_HEAD_DIM
        if dd < ROT_DIM:
            m[dd % ROT_HALF, lane] = 1.0
        if dd < ROT_HALF:
            m[ROT_HALF + dd, LANES + lane] = -1.0
        elif dd < ROT_DIM:
            m[ROT_HALF + dd - ROT_HALF, 2 * LANES + lane] = 1.0
    return np.concatenate([m, m], axis=0)


def _inproj_kernel(pos_ref, invf_ref, m2_ref, x_ref, nw_ref, w_ref, o_ref, h_ref, tab_ref, *, plan):
    j = pl.program_id(1)
    tm = x_ref.shape[0]

    @pl.when(j == 0)
    def _():
        h_ref[...] = _rmsnorm(x_ref[...], nw_ref[...]).astype(BF16)
        for gi in range(tm // LANES):
            ang = invf_ref[...] * pos_ref[gi].astype(F32)
            cs = jnp.concatenate([jnp.cos(ang) - 1.0, jnp.sin(ang)], axis=0)
            hi = cs.astype(BF16)
            lo = (cs - hi.astype(F32)).astype(BF16)
            tab_ref[gi * LANES:(gi + 1) * LANES, :] = lax.dot_general(
                jnp.concatenate([hi, lo], axis=0), m2_ref[...], TN_DIMS, preferred_element_type=F32)

    def emit(kinds):
        def body():
            r = jnp.dot(h_ref[...], w_ref[...], preferred_element_type=F32)
            cs = tab_ref[:, 0:LANES] + 1.0
            s1 = tab_ref[:, LANES:2 * LANES]
            s2 = tab_ref[:, 2 * LANES:3 * LANES]
            for gi, kind in enumerate(kinds):
                rg = r[:, gi * LANES:(gi + 1) * LANES]
                if kind is not None:
                    rg = rg * cs + pltpu.roll(rg, LANES - ROT_HALF, 1) * s1 + pltpu.roll(rg, ROT_HALF, 1) * s2
                if kind == "q":
                    rg = rg * Q_SCALE
                o_ref[:, gi * LANES:(gi + 1) * LANES] = rg.astype(o_ref.dtype)
        return body

    for jj, kinds in enumerate(plan):
        pl.when(j == jj)(emit(kinds))

    @pl.when(j >= len(plan))
    def _():
        o_ref[...] = jnp.dot(h_ref[...], w_ref[...], preferred_element_type=F32).astype(o_ref.dtype)


def _inproj(x, nw, w, pos3, invf8, m2, tm, tn):
    t, d = x.shape
    n = w.shape[1]
    kind = lambda col: "q" if col < A_WIDTH else ("k" if col < A_WIDTH + A_KV_WIDTH else None)
    plan = [tuple(kind(jj * tn + gi * LANES) for gi in range(tn // LANES)) for jj in range(n // tn)]
    plan = plan[:max(jj + 1 for jj, kinds in enumerate(plan) if any(kinds))]
    return pl.pallas_call(
        functools.partial(_inproj_kernel, plan=plan),
        grid=(t // tm, n // tn),
        in_specs=[
            pl.BlockSpec((tm // LANES, 1, LANES), lambda i, j: (i, 0, 0)),
            pl.BlockSpec(invf8.shape, lambda i, j: (0, 0)),
            pl.BlockSpec(m2.shape, lambda i, j: (0, 0)),
            pl.BlockSpec((tm, d), lambda i, j: (i, 0)),
            pl.BlockSpec((1, d), lambda i, j: (0, 0)),
            pl.BlockSpec((d, tn), lambda i, j: (0, j)),
        ],
        out_specs=pl.BlockSpec((tm, tn), lambda i, j: (i, j)),
        out_shape=jax.ShapeDtypeStruct((t, n), BF16),
        scratch_shapes=[pltpu.VMEM((tm, d), BF16), pltpu.VMEM((tm, 3 * LANES), F32)],
        compiler_params=_params("parallel", "arbitrary"),
        name="inproj",
    )(pos3, invf8, m2, x, nw, w)


def _attn_kernel(sink_ref, q_ref, k_ref, v_ref, o_ref, k_s, v_s):
    n = pl.program_id(1)
    blk = q_ref.shape[0]
    low = lax.broadcasted_iota(jnp.int32, (1, LANES), 1) < A_HEAD_DIM

    @pl.when(n == 0)
    def _():
        k_s[...] = jnp.zeros_like(k_s)
        v_s[...] = jnp.zeros_like(v_s)

    qi = lax.broadcasted_iota(jnp.int32, (blk, 2 * blk), 0)
    kj = lax.broadcasted_iota(jnp.int32, (blk, 2 * blk), 1)
    allowed = (kj > qi) & (kj <= qi + blk) & ((n > 0) | (kj >= blk))

    grp = A_HEADS // A_KV_HEADS
    for g in range(A_KV_HEADS):
        c0 = (g // 2) * LANES
        own = low if g % 2 == 0 else jnp.logical_not(low)
        kc, vc = k_ref[:, c0:c0 + LANES].astype(F32), v_ref[:, c0:c0 + LANES].astype(F32)
        k2 = jnp.where(own, kc, pltpu.roll(kc, A_HEAD_DIM, 1)).astype(BF16)
        v2 = jnp.where(own, vc, pltpu.roll(vc, A_HEAD_DIM, 1)).astype(BF16)
        k_s[g, blk:, :] = k2
        v_s[g, blk:, :] = v2
        qs = []
        for pr in range(grp // 2):
            col = (g * grp + 2 * pr) * A_HEAD_DIM
            qp = q_ref[:, col:col + LANES].astype(F32)
            qs += [jnp.where(low, qp, 0.0), jnp.where(low, 0.0, qp)]
        qst = jnp.concatenate(qs, axis=0).astype(BF16)
        s = lax.dot_general(qst, k_s[g], NT_DIMS, preferred_element_type=F32)
        es, rden = [], []
        for r in range(grp):
            sk = sink_ref[g * grp + r] * LOG2E
            sr = jnp.where(allowed, s[r * blk:(r + 1) * blk], -jnp.inf)
            m = jnp.maximum(jnp.max(sr, axis=-1, keepdims=True), sk)
            e = jnp.exp2(sr - m)
            rden.append(1.0 / (jnp.sum(e, axis=-1, keepdims=True) + jnp.exp2(sk - m)))
            es.append(e.astype(BF16))
        o = jnp.dot(jnp.concatenate(es, axis=0), v_s[g], preferred_element_type=F32)
        for pr in range(grp // 2):
            oa = o[(2 * pr) * blk:(2 * pr + 1) * blk] * rden[2 * pr]
            ob = o[(2 * pr + 1) * blk:(2 * pr + 2) * blk] * rden[2 * pr + 1]
            col = (g * grp + 2 * pr) * A_HEAD_DIM
            o_ref[:, col:col + LANES] = jnp.where(low, oa, ob).astype(o_ref.dtype)
        k_s[g, :blk, :] = k2
        v_s[g, :blk, :] = v2


def _attention(proj, sinks, batch, seq):
    nb = seq // WINDOW
    kcol = A_WIDTH // A_KV_WIDTH
    cur = lambda b, n: b * nb + n
    return pl.pallas_call(
        _attn_kernel,
        grid=(batch, nb),
        in_specs=[
            pl.BlockSpec(memory_space=pltpu.SMEM),
            pl.BlockSpec((WINDOW, A_WIDTH), lambda b, n: (cur(b, n), 0)),
            pl.BlockSpec((WINDOW, A_KV_WIDTH), lambda b, n: (cur(b, n), kcol)),
            pl.BlockSpec((WINDOW, A_KV_WIDTH), lambda b, n: (cur(b, n), kcol + 1)),
        ],
        out_specs=pl.BlockSpec((WINDOW, A_WIDTH), lambda b, n: (cur(b, n), 0)),
        out_shape=jax.ShapeDtypeStruct((batch * seq, A_WIDTH), BF16),
        scratch_shapes=[pltpu.VMEM((A_KV_HEADS, 2 * WINDOW, LANES), BF16),
                        pltpu.VMEM((A_KV_HEADS, 2 * WINDOW, LANES), BF16)],
        compiler_params=_params("parallel", "arbitrary"),
        name="swa",
    )(sinks, proj, proj, proj)


HGRN_CHUNK = 256
HGRN_QUAD = 128


def _hgrn_head(a, nw, qp, fp, v, og, st_ref, c_s, kc_s, v_s):
    cn = qp.shape[0]
    nblk = cn // SUBLANES
    quad = min(HGRN_QUAD, cn)
    bpq = quad // SUBLANES
    assert cn in (quad, 2 * quad)

    ea = jnp.exp(a - jnp.max(a, axis=0, keepdims=True))
    lb = ea[0:1] / jnp.sum(ea, axis=0, keepdims=True)

    q = qp * _sigmoid(qp)
    fg = lb + (1.0 - lb) * _sigmoid(fp)
    v_s[...] = v

    row8 = lax.broadcasted_iota(jnp.int32, (cn, LANES), 0) % SUBLANES
    w = jnp.log(fg) * LOG2E
    for sh in (1, 2, 4):
        w = w + jnp.where(row8 >= sh, pltpu.roll(w, sh, 0), 0.0)
    c_s[...] = w
    run = jnp.zeros((SUBLANES, LANES), F32)
    cb = []
    for m in range(nblk):
        cb.append(w[m * SUBLANES:(m + 1) * SUBLANES] + run)
        run = run + c_s[pl.ds(m * SUBLANES + SUBLANES - 1, 1), :]
    c = jnp.concatenate(cb, axis=0)
    c_s[...] = c
    kc = c - jnp.log(1.0 - fg) * LOG2E
    kc_s[...] = kc
    qb = [q[m * SUBLANES:(m + 1) * SUBLANES] for m in range(nblk)]
    kcb = [kc[m * SUBLANES:(m + 1) * SUBLANES] for m in range(nblk)]
    zero = jnp.zeros((SUBLANES, LANES), F32)

    def level(blocks_q, blocks_k, b):
        qt, kt = [], []
        for m in blocks_q:
            anchor = (m * SUBLANES) // (2 * b) * (2 * b) + b
            if m * SUBLANES >= anchor:
                qt.append(qb[m] * jnp.exp2(cb[m] - c_s[pl.ds(anchor - 1, 1), :]))
            else:
                qt.append(zero)
        for m in blocks_k:
            anchor = (m * SUBLANES) // (2 * b) * (2 * b) + b
            if m * SUBLANES < anchor:
                kt.append(jnp.exp2(c_s[pl.ds(anchor - 1, 1), :] - kcb[m]))
            else:
                kt.append(zero)
        return lax.dot_general(jnp.concatenate(qt, axis=0).astype(BF16), jnp.concatenate(kt, axis=0).astype(BF16),
                               NT_DIMS, preferred_element_type=F32)

    blkxor = (lax.broadcasted_iota(jnp.int32, (quad, quad), 0) ^ lax.broadcasted_iota(jnp.int32, (quad, quad), 1))
    sq = []
    for qd in range(cn // quad):
        blocks = list(range(qd * bpq, (qd + 1) * bpq))
        bs = []
        b = SUBLANES
        while b < quad:
            bs.append(b)
            b *= 2
        sc = level(blocks, blocks, bs[-1])
        for b in reversed(bs[:-1]):
            sc = jnp.where(blkxor < 2 * b, level(blocks, blocks, b), sc)
        sq.append(sc.astype(BF16))

    vb = v.astype(BF16)
    if cn == quad:
        o = jnp.dot(sq[0], vb, preferred_element_type=F32)
    else:
        lo_blocks, hi_blocks = list(range(bpq)), list(range(bpq, nblk))
        cross = level(hi_blocks, lo_blocks, quad).astype(BF16)
        o = jnp.concatenate([
            jnp.dot(sq[0], vb[:quad], preferred_element_type=F32),
            jnp.dot(jnp.concatenate([cross, sq[1]], axis=1), vb, preferred_element_type=F32)], axis=0)

    r8 = lax.broadcasted_iota(jnp.int32, (SUBLANES, LANES), 0)
    od = [zero] * nblk
    for j in range(SUBLANES):
        for m in range(nblk):
            row = pl.ds(m * SUBLANES + j, 1)
            dec = jnp.exp2(jnp.where(r8 >= j, cb[m] - kc_s[row, :], -jnp.inf))
            od[m] = od[m] + jnp.sum(qb[m] * dec, axis=-1, keepdims=True) * v_s[row, :]

    st = st_ref[...]
    o = (o + jnp.concatenate(od, axis=0)
         + lax.dot_general((q * jnp.exp2(c)).astype(BF16), st.astype(BF16), NT_DIMS, preferred_element_type=F32))

    clast = c[cn - 1:cn]
    kl = jnp.exp2(clast - kc).astype(BF16)
    st_ref[...] = jnp.exp2(clast) * st + lax.dot_general(vb, kl, TN_DIMS, preferred_element_type=F32)

    return _rmsnorm(o, nw) * (og * _sigmoid(og))


def _hgrn_kernel(*refs, ncast):
    lbp_ref, nw_ref, q_ref, f_ref, i_ref, og_ref = refs[:6]
    w_refs, o_ref, wo_refs = refs[6:6 + ncast], refs[6 + ncast], refs[7 + ncast:7 + 2 * ncast]
    st_ref, c_s, kc_s, v_s = refs[7 + 2 * ncast:]
    for w_ref, wo_ref in zip(w_refs, wo_refs):
        wo_ref[...] = w_ref[...].astype(wo_ref.dtype)

    @pl.when(pl.program_id(2) == 0)
    def _():
        st_ref[...] = jnp.zeros_like(st_ref)

    for hh in range(q_ref.shape[1] // B_DIM):
        sl = slice(hh * B_DIM, (hh + 1) * B_DIM)
        o_ref[:, sl] = _hgrn_head(
            lbp_ref[:, sl], nw_ref[:, sl], q_ref[:, sl].astype(F32), f_ref[:, sl].astype(F32),
            i_ref[:, sl].astype(F32), og_ref[:, sl].astype(F32),
            st_ref.at[hh], c_s.at[hh], kc_s.at[hh], v_s.at[hh]).astype(o_ref.dtype)


BF16_ROWS = 2 * SUBLANES


def _hgrn(proj, lbp, nw, batch, seq, hps, weights):
    cn = min(HGRN_CHUNK, seq)
    nc = seq // cn
    ng = B_HEADS // hps
    wd = hps * B_DIM
    base = (A_WIDTH + 2 * A_KV_WIDTH) // wd
    col = lambda part: (lambda b, h, t: (b * nc + t, base + part * ng + h))
    rows = batch * ng * nc * BF16_ROWS
    slabs = [wt.reshape(rows, wt.size // rows) for wt in weights]
    assert all(sl.shape[1] % LANES == 0 for sl in slabs)
    slab_spec = lambda sl: pl.BlockSpec((BF16_ROWS, sl.shape[1]), lambda b, h, t: ((b * ng + h) * nc + t, 0))
    outs = pl.pallas_call(
        functools.partial(_hgrn_kernel, ncast=len(slabs)),
        grid=(batch, ng, nc),
        in_specs=[
            pl.BlockSpec((lbp.shape[0], wd), lambda b, h, t: (0, h)),
            pl.BlockSpec((1, wd), lambda b, h, t: (0, h)),
            pl.BlockSpec((cn, wd), col(0)),
            pl.BlockSpec((cn, wd), col(1)),
            pl.BlockSpec((cn, wd), col(2)),
            pl.BlockSpec((cn, wd), col(3)),
            *[slab_spec(sl) for sl in slabs],
        ],
        out_specs=[pl.BlockSpec((cn, wd), lambda b, h, t: (b * nc + t, h)), *[slab_spec(sl) for sl in slabs]],
        out_shape=[jax.ShapeDtypeStruct((batch * seq, B_WIDTH), BF16),
                   *[jax.ShapeDtypeStruct(sl.shape, BF16) for sl in slabs]],
        scratch_shapes=[pltpu.VMEM((hps, B_DIM, B_DIM), F32), pltpu.VMEM((hps, cn, B_DIM), F32),
                        pltpu.VMEM((hps, cn, B_DIM), F32), pltpu.VMEM((hps, cn, B_DIM), F32)],
        compiler_params=_params("parallel", "parallel", "arbitrary"),
        name="hgrn2",
    )(lbp, nw, proj, proj, proj, proj, *slabs)
    return outs[0], [o.reshape(wt.shape) for o, wt in zip(outs[1:], weights)]


def _merge_kernel(*refs, nchunk):
    x_ref, oa_ref, ob_ref = refs[:3]
    ga_refs, gb_refs = refs[3:3 + nchunk], refs[3 + nchunk:3 + 2 * nchunk]
    wa_ref, wb_ref, wo_ref, o_ref = refs[3 + 2 * nchunk:]
    tn = ga_refs[0].shape[1]
    o_ref[...] = x_ref[...]
    oa, ob = oa_ref[...], ob_ref[...]
    for c in range(nchunk):
        sl = slice(c * tn, (c + 1) * tn)
        ma = jnp.dot(oa, wa_ref[:, sl], preferred_element_type=F32)
        mb = jnp.dot(ob, wb_ref[:, sl], preferred_element_type=F32)
        merged = _sigmoid(ga_refs[c][...].astype(F32)) * ma + _sigmoid(gb_refs[c][...].astype(F32)) * mb
        o_ref[...] += jnp.dot(merged.astype(BF16), wo_ref[sl, :], preferred_element_type=F32)


def _merge(x, out_a, out_b, proj, wa, wb, wo, tm, tn):
    t, d = x.shape
    nchunk = d // tn
    ga0 = (proj.shape[1] - 2 * d) // tn
    gb0 = (proj.shape[1] - d) // tn
    gate = lambda c0: [pl.BlockSpec((tm, tn), functools.partial(lambda i, cc: (i, cc), cc=c0 + c))
                       for c in range(nchunk)]
    return pl.pallas_call(
        functools.partial(_merge_kernel, nchunk=nchunk),
        grid=(t // tm,),
        in_specs=[
            pl.BlockSpec((tm, d), lambda i: (i, 0)),
            pl.BlockSpec((tm, A_WIDTH), lambda i: (i, 0)),
            pl.BlockSpec((tm, B_WIDTH), lambda i: (i, 0)),
            *gate(ga0), *gate(gb0),
            _resident(wa.shape), _resident(wb.shape), _resident(wo.shape),
        ],
        out_specs=pl.BlockSpec((tm, d), lambda i: (i, 0)),
        out_shape=jax.ShapeDtypeStruct((t, d), F32),
        compiler_params=_params("parallel"),
        name="merge",
    )(x, out_a, out_b, *([proj] * (2 * nchunk)), wa, wb, wo)


PLE_SUBTILES = 2


def _ple_kernel(x_ref, p_ref, nw_ref, fw_ref, wg_ref, wp_ref, o_ref, *, tn):
    d = x_ref.shape[1]
    rs = x_ref.shape[0] // PLE_SUBTILES
    for s in range(PLE_SUBTILES):
        rows = slice(s * rs, (s + 1) * rs)
        x = x_ref[rows, :]
        h = _rmsnorm(x, nw_ref[...]).astype(BF16)
        pb = p_ref[rows, :].astype(BF16)
        ss = jnp.zeros((rs, 1), F32)
        for c in range(d // tn):
            sl = slice(c * tn, (c + 1) * tn)
            g = _sigmoid(jnp.dot(h, wg_ref[:, sl], preferred_element_type=F32))
            y = x[:, sl] + g * jnp.dot(pb, wp_ref[:, sl], preferred_element_type=F32)
            ss = ss + jnp.sum(y * y, axis=-1, keepdims=True)
            o_ref[rows, sl] = y
        o_ref[rows, :] = o_ref[rows, :] * lax.rsqrt(ss * (1.0 / d) + EPS) * fw_ref[...]


def _ple(x, p, nw, fw, wg, wp, tm, tn):
    t, d = x.shape
    pd = p.shape[1]
    return pl.pallas_call(
        functools.partial(_ple_kernel, tn=tn),
        grid=(t // tm,),
        in_specs=[
            pl.BlockSpec((tm, d), lambda i: (i, 0)),
            pl.BlockSpec((tm, pd), lambda i: (i, 0)),
            pl.BlockSpec((1, d), lambda i: (0, 0)),
            pl.BlockSpec((1, d), lambda i: (0, 0)),
            _resident(wg.shape), _resident(wp.shape),
        ],
        out_specs=pl.BlockSpec((tm, d), lambda i: (i, 0)),
        out_shape=jax.ShapeDtypeStruct((t, d), F32),
        compiler_params=_params("parallel"),
        name="ple",
    )(x, p, nw, fw, wg, wp)


def _tiles(t, d, ff):
    return dict(ffn_tm=min(1024, t), ffn_tf=min(512, ff), proj_tm=min(1024, t), proj_tn=512,
                merge_tm=min(512, t), merge_tn=min(512, d), ple_tm=min(512, t), ple_tn=min(512, d),
                hgrn_heads=4)


def kernel(x, p, positions, ffn1_norm, ffn1_w_gate, ffn1_w_up, ffn1_w_down, mix_norm, w_in, attn_sinks, hgrn_lower_bound, hgrn_norm, w_up_a, w_up_b, w_out, ffn2_norm, ffn2_w_gate, ffn2_w_up, ffn2_w_down, ple_norm, ple_w_gate, ple_w_proj, final_norm):
    batch, seq, d = x.shape
    assert ffn1_norm.shape[0] == 1, "single-layer stack"
    t = batch * seq
    ts = _tiles(t, d, ffn1_w_gate.shape[-1])
    bf = lambda wt: wt.astype(BF16)
    row = lambda wt: wt.reshape(1, -1).astype(F32)

    inv_freq = jnp.power(jnp.float32(ROPE_THETA), -jnp.arange(0, ROT_DIM, 2, dtype=F32) / ROT_DIM)
    invf8 = jnp.broadcast_to(inv_freq[:, None], (ROT_HALF, LANES)).astype(F32)
    m2 = jnp.asarray(_rope_matrix(), BF16)
    pos3 = positions.reshape(t // LANES, 1, LANES)

    x0 = x.reshape(t, d)
    x1 = _ffn(x0, row(ffn1_norm[0]), bf(ffn1_w_gate[0]), bf(ffn1_w_up[0]), bf(ffn1_w_down[0]),
              ts["ffn_tm"], ts["ffn_tf"])
    proj = _inproj(x1, row(mix_norm[0]), bf(w_in[0]), pos3, invf8, m2, ts["proj_tm"], ts["proj_tn"])
    out_a = _attention(proj, attn_sinks[0].astype(F32), batch, seq)
    later = [w_up_a[0], w_up_b[0], w_out[0], ffn2_w_gate[0], ffn2_w_up[0], ffn2_w_down[0], ple_w_gate[0], ple_w_proj[0]]
    out_b, (wa, wb, wo, wg2, wu2, wd2, wpg, wpp) = _hgrn(
        proj, hgrn_lower_bound.astype(F32), row(hgrn_norm[0]), batch, seq, ts["hgrn_heads"], later)
    x2 = _merge(x1, out_a, out_b, proj, wa, wb, wo, ts["merge_tm"], ts["merge_tn"])
    x3 = _ffn(x2, row(ffn2_norm[0]), wg2, wu2, wd2, ts["ffn_tm"], ts["ffn_tf"])
    out = _ple(x3, p[0].reshape(t, -1), row(ple_norm[0]), row(final_norm), wpg, wpp, ts["ple_tm"], ts["ple_tn"])
    return out.reshape(batch, seq, d)
```

```python
import functools

import jax
import jax.numpy as jnp
import numpy as np
from jax import lax
from jax.experimental import pallas as pl
from jax.experimental.pallas import tpu as pltpu

EPS = 1e-6
A_HEADS = 16
A_KV_HEADS = 4
A_HEAD_DIM = 64
A_WIDTH = A_HEADS * A_HEAD_DIM
A_KV_WIDTH = A_KV_HEADS * A_HEAD_DIM
WINDOW = 128
ROT_DIM = A_HEAD_DIM // 4
ROT_HALF = ROT_DIM // 2
ROPE_THETA = 500000.0
B_HEADS = 8
B_DIM = 128
B_WIDTH = B_HEADS * B_DIM

LANES = 128
SUBLANES = 8
BF16_ROWS = 2 * SUBLANES
PROJ_TILE = 512
VMEM_LIMIT = 56 * 1024 * 1024
LOG2E = 1.4426950408889634
Q_SCALE = A_HEAD_DIM ** -0.5 * LOG2E

F32 = jnp.float32
BF16 = jnp.bfloat16
NT_DIMS = (((1,), (1,)), ((), ()))
TN_DIMS = (((0,), (0,)), ((), ()))


def _params(*sem):
    return pltpu.CompilerParams(dimension_semantics=sem, vmem_limit_bytes=VMEM_LIMIT)


def _resident(shape):
    return pl.BlockSpec(shape, lambda *_: (0,) * len(shape), pipeline_mode=pl.Buffered(1))


def _rmsnorm(x, w):
    return x * lax.rsqrt(jnp.mean(x * x, axis=-1, keepdims=True) + EPS) * w


def _sigmoid(x):
    return 1.0 / (1.0 + jnp.exp(-x))


def _ffn_kernel(x_ref, nw_ref, wg_ref, wu_ref, wd_ref, *rest):
    if len(rest) == 2:
        o_ref, h_ref = rest
    else:
        wn_ref, o_ref, wno_ref, h_ref = rest
        tn = wno_ref.shape[2]
        for jt in range(wno_ref.shape[0]):
            wno_ref[jt] = wn_ref[:, jt * tn:(jt + 1) * tn].astype(wno_ref.dtype)

    @pl.when(pl.program_id(1) == 0)
    def _():
        x = x_ref[...]
        h_ref[...] = _rmsnorm(x, nw_ref[...]).astype(BF16)
        o_ref[...] = x

    h = h_ref[...]
    g = jnp.dot(h, wg_ref[...], preferred_element_type=F32)
    u = jnp.dot(h, wu_ref[...], preferred_element_type=F32)
    a = (0.5 * g * _sigmoid(g) * u).astype(BF16)
    o_ref[...] += jnp.dot(a, wd_ref[...], preferred_element_type=F32)


def _ffn(x, nw, wg, wu, wd, tm, tf, w_next=None, tn=None):
    t, d = x.shape
    ff = wg.shape[1]
    nt, nf = t // tm, ff // tf
    in_specs = [
        pl.BlockSpec((tm, d), lambda i, f: (i, 0)),
        pl.BlockSpec((1, d), lambda i, f: (0, 0)),
        pl.BlockSpec((d, tf), lambda i, f: (0, f)),
        pl.BlockSpec((d, tf), lambda i, f: (0, f)),
        pl.BlockSpec((tf, d), lambda i, f: (f, 0)),
    ]
    out_specs = pl.BlockSpec((tm, d), lambda i, f: (i, 0))
    out_shape = jax.ShapeDtypeStruct((t, d), F32)
    args = (x, nw, wg, wu, wd)
    if w_next is not None:
        _, r, c = w_next.shape
        k = max(kk for kk in range(1, nf + 1) if r % (nt * kk) == 0 and (r // (nt * kk)) % BF16_ROWS == 0)
        rps = r // (nt * k)
        slab = lambda i, f: (0, i * k + jnp.minimum(f, k - 1), 0)
        in_specs.append(pl.BlockSpec((None, rps, c), slab))
        out_specs = [out_specs, pl.BlockSpec((c // tn, rps, tn), slab)]
        out_shape = [out_shape, jax.ShapeDtypeStruct((c // tn, r, tn), BF16)]
        args += (w_next,)
    return pl.pallas_call(
        _ffn_kernel,
        grid=(nt, nf),
        in_specs=in_specs,
        out_specs=out_specs,
        out_shape=out_shape,
        scratch_shapes=[pltpu.VMEM((tm, d), BF16)],
        compiler_params=_params("parallel", "arbitrary"),
        name="ffn",
    )(*args)


def _rope_matrix():
    m = np.zeros((2 * ROT_HALF, 3 * LANES), np.float32)
    for lane in range(LANES):
        dd = lane % A_HEAD_DIM
        if dd < ROT_DIM:
            m[dd % ROT_HALF, lane] = 1.0
        if dd < ROT_HALF:
            m[ROT_HALF + dd, LANES + lane] = -1.0
        elif dd < ROT_DIM:
            m[ROT_HALF + dd - ROT_HALF, 2 * LANES + lane] = 1.0
    return np.concatenate([m, m], axis=0)


def _inproj_kernel(pos_ref, invf_ref, m2_ref, x_ref, nw_ref, w_ref, o_ref, h_ref, tab_ref, *, plan):
    j = pl.program_id(1)
    tm = x_ref.shape[0]

    @pl.when(j == 0)
    def _():
        h_ref[...] = _rmsnorm(x_ref[...], nw_ref[...]).astype(BF16)
        for gi in range(tm // LANES):
            ang = invf_ref[...] * pos_ref[gi].astype(F32)
            cs = jnp.concatenate([jnp.cos(ang) - 1.0, jnp.sin(ang)], axis=0)
            hi = cs.astype(BF16)
            lo = (cs - hi.astype(F32)).astype(BF16)
            tab_ref[gi * LANES:(gi + 1) * LANES, :] = lax.dot_general(
                jnp.concatenate([hi, lo], axis=0), m2_ref[...], TN_DIMS, preferred_element_type=F32)

    def emit(kinds):
        def body():
            r = jnp.dot(h_ref[...], w_ref[...], preferred_element_type=F32)
            cs = tab_ref[:, 0:LANES] + 1.0
            s1 = tab_ref[:, LANES:2 * LANES]
            s2 = tab_ref[:, 2 * LANES:3 * LANES]
            for gi, kind in enumerate(kinds):
                rg = r[:, gi * LANES:(gi + 1) * LANES]
                if kind is not None:
                    rg = rg * cs + pltpu.roll(rg, LANES - ROT_HALF, 1) * s1 + pltpu.roll(rg, ROT_HALF, 1) * s2
                if kind == "q":
                    rg = rg * Q_SCALE
                o_ref[:, gi * LANES:(gi + 1) * LANES] = rg.astype(o_ref.dtype)
        return body

    for jj, kinds in enumerate(plan):
        pl.when(j == jj)(emit(kinds))

    @pl.when(j >= len(plan))
    def _():
        o_ref[...] = jnp.dot(h_ref[...], w_ref[...], preferred_element_type=F32).astype(o_ref.dtype)


def _inproj(x, nw, w, pos3, invf8, m2, tm):
    t, d = x.shape
    ntile, _, tn = w.shape
    kind = lambda col: "q" if col < A_WIDTH else ("k" if col < A_WIDTH + A_KV_WIDTH else None)
    plan = [tuple(kind(jj * tn + gi * LANES) for gi in range(tn // LANES)) for jj in range(ntile)]
    plan = plan[:max(jj + 1 for jj, kinds in enumerate(plan) if any(kinds))]
    return pl.pallas_call(
        functools.partial(_inproj_kernel, plan=plan),
        grid=(t // tm, ntile),
        in_specs=[
            pl.BlockSpec((tm // LANES, 1, LANES), lambda i, j: (i, 0, 0)),
            pl.BlockSpec(invf8.shape, lambda i, j: (0, 0)),
            pl.BlockSpec(m2.shape, lambda i, j: (0, 0)),
            pl.BlockSpec((tm, d), lambda i, j: (i, 0)),
            pl.BlockSpec((1, d), lambda i, j: (0, 0)),
            pl.BlockSpec((None, d, tn), lambda i, j: (j, 0, 0)),
        ],
        out_specs=pl.BlockSpec((None, tm, tn), lambda i, j: (j, i, 0)),
        out_shape=jax.ShapeDtypeStruct((ntile, t, tn), BF16),
        scratch_shapes=[pltpu.VMEM((tm, d), BF16), pltpu.VMEM((tm, 3 * LANES), F32)],
        compiler_params=_params("parallel", "arbitrary"),
        name="inproj",
    )(pos3, invf8, m2, x, nw, w)


def _attn_kernel(sink_ref, *refs):
    *q_refs, kv_ref, o_ref, k_s, v_s = refs
    n = pl.program_id(1)
    blk, pt = kv_ref.shape
    low = lax.broadcasted_iota(jnp.int32, (1, LANES), 1) < A_HEAD_DIM

    @pl.when(n == 0)
    def _():
        k_s[...] = jnp.zeros_like(k_s)
        v_s[...] = jnp.zeros_like(v_s)

    qi = lax.broadcasted_iota(jnp.int32, (blk, 2 * blk), 0)
    kj = lax.broadcasted_iota(jnp.int32, (blk, 2 * blk), 1)
    allowed = (kj > qi) & (kj <= qi + blk) & ((n > 0) | (kj >= blk))

    grp = A_HEADS // A_KV_HEADS
    for g in range(A_KV_HEADS):
        c0 = (g // 2) * LANES
        own = low if g % 2 == 0 else jnp.logical_not(low)
        kc = kv_ref[:, c0:c0 + LANES].astype(F32)
        vc = kv_ref[:, A_KV_WIDTH + c0:A_KV_WIDTH + c0 + LANES].astype(F32)
        k2 = jnp.where(own, kc, pltpu.roll(kc, A_HEAD_DIM, 1)).astype(BF16)
        v2 = jnp.where(own, vc, pltpu.roll(vc, A_HEAD_DIM, 1)).astype(BF16)
        k_s[g, blk:, :] = k2
        v_s[g, blk:, :] = v2
        qs = []
        for pr in range(grp // 2):
            col = (g * grp + 2 * pr) * A_HEAD_DIM
            qp = q_refs[col // pt][:, col % pt:col % pt + LANES].astype(F32)
            qs += [jnp.where(low, qp, 0.0), jnp.where(low, 0.0, qp)]
        qst = jnp.concatenate(qs, axis=0).astype(BF16)
        s = lax.dot_general(qst, k_s[g], NT_DIMS, preferred_element_type=F32)
        es, rden = [], []
        for r in range(grp):
            sk = sink_ref[g * grp + r] * LOG2E
            sr = jnp.where(allowed, s[r * blk:(r + 1) * blk], -jnp.inf)
            m = jnp.maximum(jnp.max(sr, axis=-1, keepdims=True), sk)
            e = jnp.exp2(sr - m)
            rden.append(1.0 / (jnp.sum(e, axis=-1, keepdims=True) + jnp.exp2(sk - m)))
            es.append(e.astype(BF16))
        o = jnp.dot(jnp.concatenate(es, axis=0), v_s[g], preferred_element_type=F32)
        for pr in range(grp // 2):
            oa = o[(2 * pr) * blk:(2 * pr + 1) * blk] * rden[2 * pr]
            ob = o[(2 * pr + 1) * blk:(2 * pr + 2) * blk] * rden[2 * pr + 1]
            col = (g * grp + 2 * pr) * A_HEAD_DIM
            o_ref[:, col:col + LANES] = jnp.where(low, oa, ob).astype(o_ref.dtype)
        k_s[g, :blk, :] = k2
        v_s[g, :blk, :] = v2


def _proj_spec(rows, tile_fn, row_fn):
    return pl.BlockSpec((None, rows, PROJ_TILE), lambda *g: (tile_fn(*g), row_fn(*g), 0))


def _attention(proj, sinks, batch, seq):
    nb = seq // WINDOW
    assert A_WIDTH % PROJ_TILE == 0 and 2 * A_KV_WIDTH == PROJ_TILE
    nq = A_WIDTH // PROJ_TILE
    cur = lambda b, n: b * nb + n
    return pl.pallas_call(
        _attn_kernel,
        grid=(batch, nb),
        in_specs=[
            pl.BlockSpec(memory_space=pltpu.SMEM),
            *[_proj_spec(WINDOW, functools.partial(lambda b, n, jq: jq, jq=jq), cur) for jq in range(nq + 1)],
        ],
        out_specs=pl.BlockSpec((WINDOW, A_WIDTH), lambda b, n: (cur(b, n), 0)),
        out_shape=jax.ShapeDtypeStruct((batch * seq, A_WIDTH), BF16),
        scratch_shapes=[pltpu.VMEM((A_KV_HEADS, 2 * WINDOW, LANES), BF16),
                        pltpu.VMEM((A_KV_HEADS, 2 * WINDOW, LANES), BF16)],
        compiler_params=_params("parallel", "arbitrary"),
        name="swa",
    )(sinks, *([proj] * (nq + 1)))


HGRN_CHUNK = 256
HGRN_QUAD = 128


def _hgrn_head(a, nw, qp, fp, v, og, st_ref, c_s, kc_s, v_s):
    cn = qp.shape[0]
    nblk = cn // SUBLANES
    quad = min(HGRN_QUAD, cn)
    bpq = quad // SUBLANES
    assert cn in (quad, 2 * quad)

    ea = jnp.exp(a - jnp.max(a, axis=0, keepdims=True))
    lb = ea[0:1] / jnp.sum(ea, axis=0, keepdims=True)

    q = qp * _sigmoid(qp)
    fg = lb + (1.0 - lb) * _sigmoid(fp)
    v_s[...] = v

    row8 = lax.broadcasted_iota(jnp.int32, (cn, LANES), 0) % SUBLANES
    w = jnp.log(fg) * LOG2E
    for sh in (1, 2, 4):
        w = w + jnp.where(row8 >= sh, pltpu.roll(w, sh, 0), 0.0)
    c_s[...] = w
    run = jnp.zeros((SUBLANES, LANES), F32)
    cb = []
    for m in range(nblk):
        cb.append(w[m * SUBLANES:(m + 1) * SUBLANES] + run)
        run = run + c_s[pl.ds(m * SUBLANES + SUBLANES - 1, 1), :]
    c = jnp.concatenate(cb, axis=0)
    c_s[...] = c
    kc = c - jnp.log(1.0 - fg) * LOG2E
    kc_s[...] = kc
    qb = [q[m * SUBLANES:(m + 1) * SUBLANES] for m in range(nblk)]
    kcb = [kc[m * SUBLANES:(m + 1) * SUBLANES] for m in range(nblk)]
    zero = jnp.zeros((SUBLANES, LANES), F32)

    def level(blocks_q, blocks_k, b):
        qt, kt = [], []
        for m in blocks_q:
            anchor = (m * SUBLANES) // (2 * b) * (2 * b) + b
            if m * SUBLANES >= anchor:
                qt.append(qb[m] * jnp.exp2(cb[m] - c_s[pl.ds(anchor - 1, 1), :]))
            else:
                qt.append(zero)
        for m in blocks_k:
            anchor = (m * SUBLANES) // (2 * b) * (2 * b) + b
            if m * SUBLANES < anchor:
                kt.append(jnp.exp2(c_s[pl.ds(anchor - 1, 1), :] - kcb[m]))
            else:
                kt.append(zero)
        return lax.dot_general(jnp.concatenate(qt, axis=0).astype(BF16), jnp.concatenate(kt, axis=0).astype(BF16),
                               NT_DIMS, preferred_element_type=F32)

    blkxor = (lax.broadcasted_iota(jnp.int32, (quad, quad), 0) ^ lax.broadcasted_iota(jnp.int32, (quad, quad), 1))
    sq = []
    for qd in range(cn // quad):
        blocks = list(range(qd * bpq, (qd + 1) * bpq))
        bs = []
        b = SUBLANES
        while b < quad:
            bs.append(b)
            b *= 2
        sc = level(blocks, blocks, bs[-1])
        for b in reversed(bs[:-1]):
            sc = jnp.where(blkxor < 2 * b, level(blocks, blocks, b), sc)
        sq.append(sc.astype(BF16))

    vb = v.astype(BF16)
    if cn == quad:
        o = jnp.dot(sq[0], vb, preferred_element_type=F32)
    else:
        lo_blocks, hi_blocks = list(range(bpq)), list(range(bpq, nblk))
        cross = level(hi_blocks, lo_blocks, quad).astype(BF16)
        o = jnp.concatenate([
            jnp.dot(sq[0], vb[:quad], preferred_element_type=F32),
            jnp.dot(jnp.concatenate([cross, sq[1]], axis=1), vb, preferred_element_type=F32)], axis=0)

    r8 = lax.broadcasted_iota(jnp.int32, (SUBLANES, LANES), 0)
    od = [zero] * nblk
    for j in range(SUBLANES):
        for m in range(nblk):
            row = pl.ds(m * SUBLANES + j, 1)
            dec = jnp.exp2(jnp.where(r8 >= j, cb[m] - kc_s[row, :], -jnp.inf))
            od[m] = od[m] + jnp.sum(qb[m] * dec, axis=-1, keepdims=True) * v_s[row, :]

    st = st_ref[...]
    o = (o + jnp.concatenate(od, axis=0)
         + lax.dot_general((q * jnp.exp2(c)).astype(BF16), st.astype(BF16), NT_DIMS, preferred_element_type=F32))

    clast = c[cn - 1:cn]
    kl = jnp.exp2(clast - kc).astype(BF16)
    st_ref[...] = jnp.exp2(clast) * st + lax.dot_general(vb, kl, TN_DIMS, preferred_element_type=F32)

    return _rmsnorm(o, nw) * (og * _sigmoid(og))


def _hgrn_kernel(*refs, ncast):
    lbp_ref, nw_ref, q_ref, f_ref, i_ref, og_ref = refs[:6]
    w_refs, o_ref, wo_refs = refs[6:6 + ncast], refs[6 + ncast], refs[7 + ncast:7 + 2 * ncast]
    st_ref, c_s, kc_s, v_s = refs[7 + 2 * ncast:]
    for w_ref, wo_ref in zip(w_refs, wo_refs):
        wo_ref[...] = w_ref[...].astype(wo_ref.dtype)

    @pl.when(pl.program_id(2) == 0)
    def _():
        st_ref[...] = jnp.zeros_like(st_ref)

    for hh in range(q_ref.shape[1] // B_DIM):
        sl = slice(hh * B_DIM, (hh + 1) * B_DIM)
        o_ref[:, sl] = _hgrn_head(
            lbp_ref[:, sl], nw_ref[:, sl], q_ref[:, sl].astype(F32), f_ref[:, sl].astype(F32),
            i_ref[:, sl].astype(F32), og_ref[:, sl].astype(F32),
            st_ref.at[hh], c_s.at[hh], kc_s.at[hh], v_s.at[hh]).astype(o_ref.dtype)


def _slab_blocks(r, c, nsteps):
    for ncb in range(1, nsteps + 1):
        if nsteps % ncb or c % ncb or r % (nsteps // ncb):
            continue
        br, bc = r // (nsteps // ncb), c // ncb
        if br % BF16_ROWS == 0 and bc % LANES == 0:
            return br, bc, ncb
    raise ValueError(f"no slab split of {(r, c)} over {nsteps} steps")


def _hgrn(proj, lbp, nw, batch, seq, hps, weights):
    cn = min(HGRN_CHUNK, seq)
    nc = seq // cn
    ng = B_HEADS // hps
    wd = hps * B_DIM
    assert wd == PROJ_TILE
    base = (A_WIDTH + 2 * A_KV_WIDTH) // PROJ_TILE
    tok = lambda b, h, t: b * nc + t
    part = lambda pi: _proj_spec(cn, functools.partial(lambda b, h, t, pi: base + pi * ng + h, pi=pi), tok)
    step = lambda b, h, t: (b * ng + h) * nc + t
    splits = [_slab_blocks(wt.shape[1], wt.shape[2], batch * ng * nc) for wt in weights]
    w_in = [pl.BlockSpec((None, br, bc), functools.partial(
        lambda b, h, t, ncb: (0, step(b, h, t) // ncb, step(b, h, t) % ncb), ncb=ncb)) for br, bc, ncb in splits]
    w_out = [pl.BlockSpec((br, bc), functools.partial(
        lambda b, h, t, ncb: (step(b, h, t) // ncb, step(b, h, t) % ncb), ncb=ncb)) for br, bc, ncb in splits]
    outs = pl.pallas_call(
        functools.partial(_hgrn_kernel, ncast=len(weights)),
        grid=(batch, ng, nc),
        in_specs=[
            pl.BlockSpec((lbp.shape[0], wd), lambda b, h, t: (0, h)),
            pl.BlockSpec((1, wd), lambda b, h, t: (0, h)),
            part(0), part(1), part(2), part(3),
            *w_in,
        ],
        out_specs=[pl.BlockSpec((cn, wd), lambda b, h, t: (b * nc + t, h)), *w_out],
        out_shape=[jax.ShapeDtypeStruct((batch * seq, B_WIDTH), BF16),
                   *[jax.ShapeDtypeStruct(wt.shape[1:], BF16) for wt in weights]],
        scratch_shapes=[pltpu.VMEM((hps, B_DIM, B_DIM), F32), pltpu.VMEM((hps, cn, B_DIM), F32),
                        pltpu.VMEM((hps, cn, B_DIM), F32), pltpu.VMEM((hps, cn, B_DIM), F32)],
        compiler_params=_params("parallel", "parallel", "arbitrary"),
        name="hgrn2",
    )(lbp, nw, proj, proj, proj, proj, *weights)
    return outs[0], list(outs[1:])


def _merge_kernel(*refs, nchunk):
    x_ref, oa_ref, ob_ref = refs[:3]
    ga_refs, gb_refs = refs[3:3 + nchunk], refs[3 + nchunk:3 + 2 * nchunk]
    wa_ref, wb_ref, wo_ref, o_ref = refs[3 + 2 * nchunk:]
    tn = ga_refs[0].shape[1]
    o_ref[...] = x_ref[...]
    oa, ob = oa_ref[...], ob_ref[...]
    for c in range(nchunk):
        sl = slice(c * tn, (c + 1) * tn)
        ma = jnp.dot(oa, wa_ref[:, sl], preferred_element_type=F32)
        mb = jnp.dot(ob, wb_ref[:, sl], preferred_element_type=F32)
        merged = _sigmoid(ga_refs[c][...].astype(F32)) * ma + _sigmoid(gb_refs[c][...].astype(F32)) * mb
        o_ref[...] += jnp.dot(merged.astype(BF16), wo_ref[sl, :], preferred_element_type=F32)


def _merge(x, out_a, out_b, proj, wa, wb, wo, tm):
    t, d = x.shape
    tn = PROJ_TILE
    nchunk = d // tn
    ga0 = proj.shape[0] - 2 * nchunk
    gb0 = proj.shape[0] - nchunk
    gate = lambda c0: [_proj_spec(tm, functools.partial(lambda i, cc: cc, cc=c0 + c), lambda i: i)
                       for c in range(nchunk)]
    return pl.pallas_call(
        functools.partial(_merge_kernel, nchunk=nchunk),
        grid=(t // tm,),
        in_specs=[
            pl.BlockSpec((tm, d), lambda i: (i, 0)),
            pl.BlockSpec((tm, A_WIDTH), lambda i: (i, 0)),
            pl.BlockSpec((tm, B_WIDTH), lambda i: (i, 0)),
            *gate(ga0), *gate(gb0),
            _resident(wa.shape), _resident(wb.shape), _resident(wo.shape),
        ],
        out_specs=pl.BlockSpec((tm, d), lambda i: (i, 0)),
        out_shape=jax.ShapeDtypeStruct((t, d), F32),
        compiler_params=_params("parallel"),
        name="merge",
    )(x, out_a, out_b, *([proj] * (2 * nchunk)), wa, wb, wo)


PLE_SUBTILES = 2


def _ple_kernel(x_ref, p_ref, nw_ref, fw_ref, wg_ref, wp_ref, o_ref, *, tn):
    d = x_ref.shape[1]
    rs = x_ref.shape[0] // PLE_SUBTILES
    for s in range(PLE_SUBTILES):
        rows = slice(s * rs, (s + 1) * rs)
        x = x_ref[rows, :]
        h = _rmsnorm(x, nw_ref[...]).astype(BF16)
        pb = p_ref[rows, :].astype(BF16)
        ss = jnp.zeros((rs, 1), F32)
        for c in range(d // tn):
            sl = slice(c * tn, (c + 1) * tn)
            g = _sigmoid(jnp.dot(h, wg_ref[:, sl], preferred_element_type=F32))
            y = x[:, sl] + g * jnp.dot(pb, wp_ref[:, sl], preferred_element_type=F32)
            ss = ss + jnp.sum(y * y, axis=-1, keepdims=True)
            o_ref[rows, sl] = y
        o_ref[rows, :] = o_ref[rows, :] * lax.rsqrt(ss * (1.0 / d) + EPS) * fw_ref[...]


def _ple(x, p, nw, fw, wg, wp, tm, tn):
    t, d = x.shape
    pd = p.shape[1]
    return pl.pallas_call(
        functools.partial(_ple_kernel, tn=tn),
        grid=(t // tm,),
        in_specs=[
            pl.BlockSpec((tm, d), lambda i: (i, 0)),
            pl.BlockSpec((tm, pd), lambda i: (i, 0)),
            pl.BlockSpec((1, d), lambda i: (0, 0)),
            pl.BlockSpec((1, d), lambda i: (0, 0)),
            _resident(wg.shape), _resident(wp.shape),
        ],
        out_specs=pl.BlockSpec((tm, d), lambda i: (i, 0)),
        out_shape=jax.ShapeDtypeStruct((t, d), F32),
        compiler_params=_params("parallel"),
        name="ple",
    )(x, p, nw, fw, wg, wp)


def _tiles(t, d, ff):
    return dict(ffn_tm=min(1024, t), ffn_tf=min(512, ff), proj_tm=min(1024, t),
                merge_tm=min(512, t), ple_tm=min(512, t), ple_tn=min(512, d), hgrn_heads=PROJ_TILE // B_DIM)


def kernel(x, p, positions, ffn1_norm, ffn1_w_gate, ffn1_w_up, ffn1_w_down, mix_norm, w_in, attn_sinks, hgrn_lower_bound, hgrn_norm, w_up_a, w_up_b, w_out, ffn2_norm, ffn2_w_gate, ffn2_w_up, ffn2_w_down, ple_norm, ple_w_gate, ple_w_proj, final_norm):
    batch, seq, d = x.shape
    assert ffn1_norm.shape[0] == 1, "single-layer stack"
    t = batch * seq
    ts = _tiles(t, d, ffn1_w_gate.shape[-1])
    bf = lambda wt: wt.astype(BF16)
    row = lambda wt: wt.reshape(1, -1).astype(F32)

    inv_freq = jnp.power(jnp.float32(ROPE_THETA), -jnp.arange(0, ROT_DIM, 2, dtype=F32) / ROT_DIM)
    invf8 = jnp.broadcast_to(inv_freq[:, None], (ROT_HALF, LANES)).astype(F32)
    m2 = jnp.asarray(_rope_matrix(), BF16)
    pos3 = positions.reshape(t // LANES, 1, LANES)

    x0 = x.reshape(t, d)
    x1, w_in_t = _ffn(x0, row(ffn1_norm[0]), bf(ffn1_w_gate[0]), bf(ffn1_w_up[0]), bf(ffn1_w_down[0]),
                      ts["ffn_tm"], ts["ffn_tf"], w_next=w_in, tn=PROJ_TILE)
    proj = _inproj(x1, row(mix_norm[0]), w_in_t, pos3, invf8, m2, ts["proj_tm"])
    out_a = _attention(proj, attn_sinks[0].astype(F32), batch, seq)
    later = [w_up_a, w_up_b, w_out, ffn2_w_gate, ffn2_w_up, ffn2_w_down, ple_w_gate, ple_w_proj]
    out_b, (wa, wb, wo, wg2, wu2, wd2, wpg, wpp) = _hgrn(
        proj, hgrn_lower_bound.astype(F32), row(hgrn_norm[0]), batch, seq, ts["hgrn_heads"], later)
    x2 = _merge(x1, out_a, out_b, proj, wa, wb, wo, ts["merge_tm"])
    x3 = _ffn(x2, row(ffn2_norm[0]), wg2, wu2, wd2, ts["ffn_tm"], ts["ffn_tf"])
    out = _ple(x3, p[0].reshape(t, -1), row(ple_norm[0]), row(final_norm), wpg, wpp, ts["ple_tm"], ts["ple_tn"])
    return out.reshape(batch, seq, d)
```

```python
import functools

import jax
import jax.numpy as jnp
import numpy as np
from jax import lax
from jax.experimental import pallas as pl
from jax.experimental.pallas import tpu as pltpu

EPS = 1e-6
A_HEADS = 16
A_KV_HEADS = 4
A_HEAD_DIM = 64
A_WIDTH = A_HEADS * A_HEAD_DIM
A_KV_WIDTH = A_KV_HEADS * A_HEAD_DIM
WINDOW = 128
ROT_DIM = A_HEAD_DIM // 4
ROT_HALF = ROT_DIM // 2
ROPE_THETA = 500000.0
B_HEADS = 8
B_DIM = 128
B_WIDTH = B_HEADS * B_DIM

LANES = 128
SUBLANES = 8
BF16_ROWS = 2 * SUBLANES
PROJ_TILE = 512
VMEM_LIMIT = 56 * 1024 * 1024
MIXER_VMEM_LIMIT = 58 * 1024 * 1024
LOG2E = 1.4426950408889634
Q_SCALE = A_HEAD_DIM ** -0.5 * LOG2E

F32 = jnp.float32
BF16 = jnp.bfloat16
NT_DIMS = (((1,), (1,)), ((), ()))
TN_DIMS = (((0,), (0,)), ((), ()))


def _params(*sem):
    return pltpu.CompilerParams(dimension_semantics=sem, vmem_limit_bytes=VMEM_LIMIT)


def _resident(shape):
    return pl.BlockSpec(shape, lambda *_: (0,) * len(shape), pipeline_mode=pl.Buffered(1))


def _rmsnorm(x, w):
    return x * lax.rsqrt(jnp.mean(x * x, axis=-1, keepdims=True) + EPS) * w


def _sigmoid(x):
    return 1.0 / (1.0 + jnp.exp(-x))


def _ffn_kernel(x_ref, nw_ref, wg_ref, wu_ref, wd_ref, *rest):
    if len(rest) == 2:
        o_ref, h_ref = rest
    else:
        wn_ref, o_ref, wno_ref, h_ref = rest
        tn = wno_ref.shape[2]
        for jt in range(wno_ref.shape[0]):
            wno_ref[jt] = wn_ref[:, jt * tn:(jt + 1) * tn].astype(wno_ref.dtype)

    @pl.when(pl.program_id(1) == 0)
    def _():
        x = x_ref[...]
        h_ref[...] = _rmsnorm(x, nw_ref[...]).astype(BF16)
        o_ref[...] = x

    h = h_ref[...]
    g = jnp.dot(h, wg_ref[...], preferred_element_type=F32)
    u = jnp.dot(h, wu_ref[...], preferred_element_type=F32)
    a = (0.5 * g * _sigmoid(g) * u).astype(BF16)
    o_ref[...] += jnp.dot(a, wd_ref[...], preferred_element_type=F32)


def _ffn(x, nw, wg, wu, wd, tm, tf, w_next=None, tn=None):
    t, d = x.shape
    ff = wg.shape[1]
    nt, nf = t // tm, ff // tf
    in_specs = [
        pl.BlockSpec((tm, d), lambda i, f: (i, 0)),
        pl.BlockSpec((1, d), lambda i, f: (0, 0)),
        pl.BlockSpec((d, tf), lambda i, f: (0, f)),
        pl.BlockSpec((d, tf), lambda i, f: (0, f)),
        pl.BlockSpec((tf, d), lambda i, f: (f, 0)),
    ]
    out_specs = pl.BlockSpec((tm, d), lambda i, f: (i, 0))
    out_shape = jax.ShapeDtypeStruct((t, d), F32)
    args = (x, nw, wg, wu, wd)
    if w_next is not None:
        _, r, c = w_next.shape
        k = max(kk for kk in range(1, nf + 1) if r % (nt * kk) == 0 and (r // (nt * kk)) % BF16_ROWS == 0)
        rps = r // (nt * k)
        slab = lambda i, f: (0, i * k + jnp.minimum(f, k - 1), 0)
        in_specs.append(pl.BlockSpec((None, rps, c), slab))
        out_specs = [out_specs, pl.BlockSpec((c // tn, rps, tn), slab)]
        out_shape = [out_shape, jax.ShapeDtypeStruct((c // tn, r, tn), BF16)]
        args += (w_next,)
    return pl.pallas_call(
        _ffn_kernel,
        grid=(nt, nf),
        in_specs=in_specs,
        out_specs=out_specs,
        out_shape=out_shape,
        scratch_shapes=[pltpu.VMEM((tm, d), BF16)],
        compiler_params=_params("parallel", "arbitrary"),
        name="ffn",
    )(*args)


def _rope_matrix():
    m = np.zeros((2 * ROT_HALF, 3 * LANES), np.float32)
    for lane in range(LANES):
        dd = lane % A_HEAD_DIM
        if dd < ROT_DIM:
            m[dd % ROT_HALF, lane] = 1.0
        if dd < ROT_HALF:
            m[ROT_HALF + dd, LANES + lane] = -1.0
        elif dd < ROT_DIM:
            m[ROT_HALF + dd - ROT_HALF, 2 * LANES + lane] = 1.0
    return np.concatenate([m, m], axis=0)


def _rope_tables(pos_ref, invf_ref, m2_ref, tab_ref):
    for gi in range(tab_ref.shape[0] // LANES):
        ang = invf_ref[...] * pos_ref[gi].astype(F32)
        cs = jnp.concatenate([jnp.cos(ang) - 1.0, jnp.sin(ang)], axis=0)
        hi = cs.astype(BF16)
        lo = (cs - hi.astype(F32)).astype(BF16)
        tab_ref[gi * LANES:(gi + 1) * LANES, :] = lax.dot_general(
            jnp.concatenate([hi, lo], axis=0), m2_ref[...], TN_DIMS, preferred_element_type=F32)


def _rope_tile(r, tab_ref, kinds):
    cs = tab_ref[:, 0:LANES] + 1.0
    s1 = tab_ref[:, LANES:2 * LANES]
    s2 = tab_ref[:, 2 * LANES:3 * LANES]
    out = []
    for gi, kind in enumerate(kinds):
        rg = r[:, gi * LANES:(gi + 1) * LANES]
        if kind is not None:
            rg = rg * cs + pltpu.roll(rg, LANES - ROT_HALF, 1) * s1 + pltpu.roll(rg, ROT_HALF, 1) * s2
        if kind == "q":
            rg = rg * Q_SCALE
        out.append(rg)
    return jnp.concatenate(out, axis=1)


def _swa_block(sink_ref, q_at, kv, first, k_s, v_s, store):
    blk = kv.shape[0]
    low = lax.broadcasted_iota(jnp.int32, (1, LANES), 1) < A_HEAD_DIM
    qi = lax.broadcasted_iota(jnp.int32, (blk, 2 * blk), 0)
    kj = lax.broadcasted_iota(jnp.int32, (blk, 2 * blk), 1)
    allowed = (kj > qi) & (kj <= qi + blk) & (jnp.logical_not(first) | (kj >= blk))

    grp = A_HEADS // A_KV_HEADS
    for g in range(A_KV_HEADS):
        c0 = (g // 2) * LANES
        own = low if g % 2 == 0 else jnp.logical_not(low)
        kc = kv[:, c0:c0 + LANES].astype(F32)
        vc = kv[:, A_KV_WIDTH + c0:A_KV_WIDTH + c0 + LANES].astype(F32)
        k2 = jnp.where(own, kc, pltpu.roll(kc, A_HEAD_DIM, 1)).astype(BF16)
        v2 = jnp.where(own, vc, pltpu.roll(vc, A_HEAD_DIM, 1)).astype(BF16)
        k_s[g, blk:, :] = k2
        v_s[g, blk:, :] = v2
        qs = []
        for pr in range(grp // 2):
            qp = q_at((g * grp + 2 * pr) * A_HEAD_DIM).astype(F32)
            qs += [jnp.where(low, qp, 0.0), jnp.where(low, 0.0, qp)]
        qst = jnp.concatenate(qs, axis=0).astype(BF16)
        s = lax.dot_general(qst, k_s[g], NT_DIMS, preferred_element_type=F32)
        es, rden = [], []
        for r in range(grp):
            sk = sink_ref[g * grp + r] * LOG2E
            sr = jnp.where(allowed, s[r * blk:(r + 1) * blk], -jnp.inf)
            m = jnp.maximum(jnp.max(sr, axis=-1, keepdims=True), sk)
            e = jnp.exp2(sr - m)
            rden.append(1.0 / (jnp.sum(e, axis=-1, keepdims=True) + jnp.exp2(sk - m)))
            es.append(e.astype(BF16))
        o = jnp.dot(jnp.concatenate(es, axis=0), v_s[g], preferred_element_type=F32)
        for pr in range(grp // 2):
            oa = o[(2 * pr) * blk:(2 * pr + 1) * blk] * rden[2 * pr]
            ob = o[(2 * pr + 1) * blk:(2 * pr + 2) * blk] * rden[2 * pr + 1]
            store((g * grp + 2 * pr) * A_HEAD_DIM, jnp.where(low, oa, ob))
        k_s[g, :blk, :] = k2
        v_s[g, :blk, :] = v2


HGRN_CHUNK = 256
HGRN_QUAD = 128


def _hgrn_chunk(a, nw, qp, fp, v, og, st, c_s, kc_s, v_s):
    cn = qp.shape[0]
    nblk = cn // SUBLANES
    quad = min(HGRN_QUAD, cn)
    bpq = quad // SUBLANES
    assert cn in (quad, 2 * quad)

    ea = jnp.exp(a - jnp.max(a, axis=0, keepdims=True))
    lb = ea[0:1] / jnp.sum(ea, axis=0, keepdims=True)

    q = qp * _sigmoid(qp)
    fg = lb + (1.0 - lb) * _sigmoid(fp)
    v_s[...] = v

    row8 = lax.broadcasted_iota(jnp.int32, (cn, LANES), 0) % SUBLANES
    w = jnp.log(fg) * LOG2E
    for sh in (1, 2, 4):
        w = w + jnp.where(row8 >= sh, pltpu.roll(w, sh, 0), 0.0)
    c_s[...] = w
    run = jnp.zeros((SUBLANES, LANES), F32)
    cb = []
    for m in range(nblk):
        cb.append(w[m * SUBLANES:(m + 1) * SUBLANES] + run)
        run = run + c_s[pl.ds(m * SUBLANES + SUBLANES - 1, 1), :]
    c = jnp.concatenate(cb, axis=0)
    c_s[...] = c
    kc = c - jnp.log(1.0 - fg) * LOG2E
    kc_s[...] = kc
    qb = [q[m * SUBLANES:(m + 1) * SUBLANES] for m in range(nblk)]
    kcb = [kc[m * SUBLANES:(m + 1) * SUBLANES] for m in range(nblk)]
    zero = jnp.zeros((SUBLANES, LANES), F32)

    def level(blocks_q, blocks_k, b):
        qt, kt = [], []
        for m in blocks_q:
            anchor = (m * SUBLANES) // (2 * b) * (2 * b) + b
            if m * SUBLANES >= anchor:
                qt.append(qb[m] * jnp.exp2(cb[m] - c_s[pl.ds(anchor - 1, 1), :]))
            else:
                qt.append(zero)
        for m in blocks_k:
            anchor = (m * SUBLANES) // (2 * b) * (2 * b) + b
            if m * SUBLANES < anchor:
                kt.append(jnp.exp2(c_s[pl.ds(anchor - 1, 1), :] - kcb[m]))
            else:
                kt.append(zero)
        return lax.dot_general(jnp.concatenate(qt, axis=0).astype(BF16), jnp.concatenate(kt, axis=0).astype(BF16),
                               NT_DIMS, preferred_element_type=F32)

    blkxor = (lax.broadcasted_iota(jnp.int32, (quad, quad), 0) ^ lax.broadcasted_iota(jnp.int32, (quad, quad), 1))
    sq = []
    for qd in range(cn // quad):
        blocks = list(range(qd * bpq, (qd + 1) * bpq))
        bs = []
        b = SUBLANES
        while b < quad:
            bs.append(b)
            b *= 2
        sc = level(blocks, blocks, bs[-1])
        for b in reversed(bs[:-1]):
            sc = jnp.where(blkxor < 2 * b, level(blocks, blocks, b), sc)
        sq.append(sc.astype(BF16))

    vb = v.astype(BF16)
    if cn == quad:
        o = jnp.dot(sq[0], vb, preferred_element_type=F32)
    else:
        lo_blocks, hi_blocks = list(range(bpq)), list(range(bpq, nblk))
        cross = level(hi_blocks, lo_blocks, quad).astype(BF16)
        o = jnp.concatenate([
            jnp.dot(sq[0], vb[:quad], preferred_element_type=F32),
            jnp.dot(jnp.concatenate([cross, sq[1]], axis=1), vb, preferred_element_type=F32)], axis=0)

    r8 = lax.broadcasted_iota(jnp.int32, (SUBLANES, LANES), 0)
    od = [zero] * nblk
    for j in range(SUBLANES):
        for m in range(nblk):
            row = pl.ds(m * SUBLANES + j, 1)
            dec = jnp.exp2(jnp.where(r8 >= j, cb[m] - kc_s[row, :], -jnp.inf))
            od[m] = od[m] + jnp.sum(qb[m] * dec, axis=-1, keepdims=True) * v_s[row, :]

    o = (o + jnp.concatenate(od, axis=0)
         + lax.dot_general((q * jnp.exp2(c)).astype(BF16), st.astype(BF16), NT_DIMS, preferred_element_type=F32))

    clast = c[cn - 1:cn]
    kl = jnp.exp2(clast - kc).astype(BF16)
    st_next = jnp.exp2(clast) * st + lax.dot_general(vb, kl, TN_DIMS, preferred_element_type=F32)
    return _rmsnorm(o, nw) * (og * _sigmoid(og)), st_next


def _slab_blocks(r, c, nblocks):
    for ncb in range(1, nblocks + 1):
        if nblocks % ncb or c % ncb or r % (nblocks // ncb):
            continue
        br, bc = r // (nblocks // ncb), c // ncb
        if br % BF16_ROWS == 0 and bc % LANES == 0:
            return br, bc, ncb
    return None


def _cast_specs(weights, nt, nsteps):
    ins, outs = [], []
    for wt in weights:
        _, r, c = wt.shape
        k, (br, bc, ncb) = next((kk, _slab_blocks(r, c, nt * kk)) for kk in range(nsteps, 0, -1)
                                if _slab_blocks(r, c, nt * kk))
        blk = functools.partial(lambda i, j, k, ncb: divmod(i * k + jnp.minimum(j, k - 1), ncb), k=k, ncb=ncb)
        ins.append(pl.BlockSpec((None, br, bc), functools.partial(lambda i, j, blk: (0, *blk(i, j)), blk=blk)))
        outs.append(pl.BlockSpec((br, bc), blk))
    return ins, outs


def _mixer_kernel(sink_ref, pos_ref, invf_ref, m2_ref, x_ref, nw_ref, w_ref, lbp_ref, hnw_ref, *rest,
                  plan, n_hin, n_gate, ncast, nb_seq, nt_seq):
    cast_in, rest = rest[:ncast], rest[ncast:]
    g_ref, oa_ref, ob_ref = rest[:3]
    cast_out, rest = rest[3:3 + ncast], rest[3 + ncast:]
    h_ref, tab_ref, qkv_s, hin_s, k_s, v_s, st_ref, c_s, kc_s, v2_s = rest
    i, j = pl.program_id(0), pl.program_id(1)
    tm = x_ref.shape[0]
    n_rope = len(plan)
    hps = B_HEADS // n_gate
    cn = c_s.shape[1]

    for w_in_ref, w_out_ref in zip(cast_in, cast_out):
        w_out_ref[...] = w_in_ref[...].astype(w_out_ref.dtype)

    @pl.when((i == 0) & (j == 0))
    def _():
        k_s[...] = jnp.zeros_like(k_s)
        v_s[...] = jnp.zeros_like(v_s)
        st_ref[...] = jnp.zeros_like(st_ref)

    @pl.when(j == 0)
    def _():
        h_ref[...] = _rmsnorm(x_ref[...], nw_ref[...]).astype(BF16)
        _rope_tables(pos_ref, invf_ref, m2_ref, tab_ref)

    def project():
        return jnp.dot(h_ref[...], w_ref[...], preferred_element_type=F32)

    for jj, kinds in enumerate(plan):
        @pl.when(j == jj)
        def _(jj=jj, kinds=kinds):
            qkv_s[jj] = _rope_tile(project(), tab_ref, kinds).astype(BF16)

    @pl.when((j >= n_rope) & (j < n_rope + n_hin))
    def _():
        u = j - n_rope
        r = project().astype(BF16)
        for hh in range(PROJ_TILE // B_DIM):
            hin_s[u * (PROJ_TILE // B_DIM) + hh] = r[:, hh * B_DIM:(hh + 1) * B_DIM]
        rows = pl.ds(pl.multiple_of(u * WINDOW, WINDOW), WINDOW)
        first = (i * n_hin + u) % nb_seq == 0
        q_at = lambda col: qkv_s[col // PROJ_TILE, rows, col % PROJ_TILE:col % PROJ_TILE + LANES]

        def store(col, val):
            oa_ref[rows, col:col + LANES] = val.astype(oa_ref.dtype)

        _swa_block(sink_ref, q_at, qkv_s[n_rope - 1, rows, :], first, k_s, v_s, store)

    @pl.when(j >= n_rope + n_hin)
    def _():
        g_ref[...] = project().astype(g_ref.dtype)
        u = j - n_rope - n_hin
        fresh = i % nt_seq == 0
        for hh in range(hps):
            head = u * hps + hh
            st = jnp.where(fresh, 0.0, st_ref[head])
            for ci in range(tm // cn):
                rows = slice(ci * cn, (ci + 1) * cn)
                part = lambda pi: hin_s[pi * B_HEADS + head, rows, :].astype(F32)
                o, st = _hgrn_chunk(lbp_ref[head], hnw_ref[head], part(0), part(1), part(2), part(3), st,
                                    c_s.at[ci], kc_s.at[ci], v2_s.at[ci])
                ob_ref[head, rows, :] = o.astype(ob_ref.dtype)
            st_ref[head] = st


def _mixers(x, nw, w, pos3, invf8, m2, sinks, lbp, hnw, weights, tm, seq):
    t, d = x.shape
    ntile, _, tn = w.shape
    assert tn == PROJ_TILE and A_WIDTH % tn == 0 and 2 * A_KV_WIDTH == tn and seq % tm == 0
    kind = lambda col: "q" if col < A_WIDTH else ("k" if col < A_WIDTH + A_KV_WIDTH else None)
    plan = [tuple(kind(jj * tn + gi * LANES) for gi in range(tn // LANES)) for jj in range(ntile)]
    plan = plan[:max(jj + 1 for jj, kinds in enumerate(plan) if any(kinds))]
    n_rope = len(plan)
    n_hin = 4 * B_WIDTH // tn
    n_gate = ntile - n_rope - n_hin
    assert tm // WINDOW == n_hin and B_HEADS % n_gate == 0
    cn = min(HGRN_CHUNK, tm)
    nt = t // tm
    cast_in, cast_out = _cast_specs(weights, nt, ntile)
    const2 = lambda i, j: (0, 0)
    const3 = lambda i, j: (0, 0, 0)
    outs = pl.pallas_call(
        functools.partial(_mixer_kernel, plan=plan, n_hin=n_hin, n_gate=n_gate, ncast=len(weights),
                          nb_seq=seq // WINDOW, nt_seq=seq // tm),
        grid=(nt, ntile),
        in_specs=[
            pl.BlockSpec(memory_space=pltpu.SMEM),
            pl.BlockSpec((tm // LANES, 1, LANES), lambda i, j: (i, 0, 0)),
            pl.BlockSpec(invf8.shape, const2),
            pl.BlockSpec(m2.shape, const2),
            pl.BlockSpec((tm, d), lambda i, j: (i, 0)),
            pl.BlockSpec((1, d), const2),
            pl.BlockSpec((None, d, tn), lambda i, j: (j, 0, 0)),
            pl.BlockSpec(lbp.shape, const3),
            pl.BlockSpec(hnw.shape, const3),
            *cast_in,
        ],
        out_specs=[
            pl.BlockSpec((None, tm, tn), lambda i, j: (jnp.maximum(j - n_rope - n_hin, 0), i, 0)),
            pl.BlockSpec((tm, A_WIDTH), lambda i, j: (i, 0)),
            pl.BlockSpec((B_HEADS, tm, B_DIM), lambda i, j: (0, i, 0)),
            *cast_out,
        ],
        out_shape=[
            jax.ShapeDtypeStruct((n_gate, t, tn), BF16),
            jax.ShapeDtypeStruct((t, A_WIDTH), BF16),
            jax.ShapeDtypeStruct((B_HEADS, t, B_DIM), BF16),
            *[jax.ShapeDtypeStruct(wt.shape[1:], BF16) for wt in weights],
        ],
        scratch_shapes=[
            pltpu.VMEM((tm, d), BF16),
            pltpu.VMEM((tm, 3 * LANES), F32),
            pltpu.VMEM((n_rope, tm, tn), BF16),
            pltpu.VMEM((4 * B_HEADS, tm, B_DIM), BF16),
            pltpu.VMEM((A_KV_HEADS, 2 * WINDOW, LANES), BF16),
            pltpu.VMEM((A_KV_HEADS, 2 * WINDOW, LANES), BF16),
            pltpu.VMEM((B_HEADS, B_DIM, B_DIM), F32),
            pltpu.VMEM((tm // cn, cn, B_DIM), F32),
            pltpu.VMEM((tm // cn, cn, B_DIM), F32),
            pltpu.VMEM((tm // cn, cn, B_DIM), F32),
        ],
        compiler_params=pltpu.CompilerParams(dimension_semantics=("arbitrary", "arbitrary"),
                                             vmem_limit_bytes=MIXER_VMEM_LIMIT),
        name="mixers",
    )(sinks, pos3, invf8, m2, x, nw, w, lbp, hnw, *weights)
    return outs[0], outs[1], outs[2], list(outs[3:])


def _merge_kernel(*refs, nchunk):
    x_ref, oa_ref, ob_ref = refs[:3]
    ga_refs, gb_refs = refs[3:3 + nchunk], refs[3 + nchunk:3 + 2 * nchunk]
    wa_ref, wb_ref, wo_ref, o_ref = refs[3 + 2 * nchunk:]
    tn = ga_refs[0].shape[1]
    o_ref[...] = x_ref[...]
    oa = oa_ref[...]
    ob = jnp.concatenate([ob_ref[hd] for hd in range(ob_ref.shape[0])], axis=1)
    for c in range(nchunk):
        sl = slice(c * tn, (c + 1) * tn)
        ma = jnp.dot(oa, wa_ref[:, sl], preferred_element_type=F32)
        mb = jnp.dot(ob, wb_ref[:, sl], preferred_element_type=F32)
        merged = _sigmoid(ga_refs[c][...].astype(F32)) * ma + _sigmoid(gb_refs[c][...].astype(F32)) * mb
        o_ref[...] += jnp.dot(merged.astype(BF16), wo_ref[sl, :], preferred_element_type=F32)


def _merge(x, out_a, out_b, gates, wa, wb, wo, tm):
    t, d = x.shape
    tn = PROJ_TILE
    nchunk = d // tn
    gate = lambda c0: [pl.BlockSpec((None, tm, tn), functools.partial(lambda i, cc: (cc, i, 0), cc=c0 + c))
                       for c in range(nchunk)]
    return pl.pallas_call(
        functools.partial(_merge_kernel, nchunk=nchunk),
        grid=(t // tm,),
        in_specs=[
            pl.BlockSpec((tm, d), lambda i: (i, 0)),
            pl.BlockSpec((tm, A_WIDTH), lambda i: (i, 0)),
            pl.BlockSpec((B_HEADS, tm, B_DIM), lambda i: (0, i, 0)),
            *gate(0), *gate(nchunk),
            _resident(wa.shape), _resident(wb.shape), _resident(wo.shape),
        ],
        out_specs=pl.BlockSpec((tm, d), lambda i: (i, 0)),
        out_shape=jax.ShapeDtypeStruct((t, d), F32),
        compiler_params=_params("parallel"),
        name="merge",
    )(x, out_a, out_b, *([gates] * (2 * nchunk)), wa, wb, wo)


PLE_SUBTILES = 2


def _ple_kernel(x_ref, p_ref, nw_ref, fw_ref, wg_ref, wp_ref, o_ref, *, tn):
    d = x_ref.shape[1]
    rs = x_ref.shape[0] // PLE_SUBTILES
    for s in range(PLE_SUBTILES):
        rows = slice(s * rs, (s + 1) * rs)
        x = x_ref[rows, :]
        h = _rmsnorm(x, nw_ref[...]).astype(BF16)
        pb = p_ref[rows, :].astype(BF16)
        ss = jnp.zeros((rs, 1), F32)
        for c in range(d // tn):
            sl = slice(c * tn, (c + 1) * tn)
            g = _sigmoid(jnp.dot(h, wg_ref[:, sl], preferred_element_type=F32))
            y = x[:, sl] + g * jnp.dot(pb, wp_ref[:, sl], preferred_element_type=F32)
            ss = ss + jnp.sum(y * y, axis=-1, keepdims=True)
            o_ref[rows, sl] = y
        o_ref[rows, :] = o_ref[rows, :] * lax.rsqrt(ss * (1.0 / d) + EPS) * fw_ref[...]


def _ple(x, p, nw, fw, wg, wp, tm, tn):
    t, d = x.shape
    pd = p.shape[1]
    return pl.pallas_call(
        functools.partial(_ple_kernel, tn=tn),
        grid=(t // tm,),
        in_specs=[
            pl.BlockSpec((tm, d), lambda i: (i, 0)),
            pl.BlockSpec((tm, pd), lambda i: (i, 0)),
            pl.BlockSpec((1, d), lambda i: (0, 0)),
            pl.BlockSpec((1, d), lambda i: (0, 0)),
            _resident(wg.shape), _resident(wp.shape),
        ],
        out_specs=pl.BlockSpec((tm, d), lambda i: (i, 0)),
        out_shape=jax.ShapeDtypeStruct((t, d), F32),
        compiler_params=_params("parallel"),
        name="ple",
    )(x, p, nw, fw, wg, wp)


def _tiles(t, d, ff):
    return dict(ffn_tm=min(1024, t), ffn_tf=min(512, ff), mix_tm=min(1024, t),
                merge_tm=min(512, t), ple_tm=min(512, t), ple_tn=min(512, d))


def kernel(x, p, positions, ffn1_norm, ffn1_w_gate, ffn1_w_up, ffn1_w_down, mix_norm, w_in, attn_sinks, hgrn_lower_bound, hgrn_norm, w_up_a, w_up_b, w_out, ffn2_norm, ffn2_w_gate, ffn2_w_up, ffn2_w_down, ple_norm, ple_w_gate, ple_w_proj, final_norm):
    batch, seq, d = x.shape
    assert ffn1_norm.shape[0] == 1, "single-layer stack"
    t = batch * seq
    ts = _tiles(t, d, ffn1_w_gate.shape[-1])
    bf = lambda wt: wt.astype(BF16)
    row = lambda wt: wt.reshape(1, -1).astype(F32)

    inv_freq = jnp.power(jnp.float32(ROPE_THETA), -jnp.arange(0, ROT_DIM, 2, dtype=F32) / ROT_DIM)
    invf8 = jnp.broadcast_to(inv_freq[:, None], (ROT_HALF, LANES)).astype(F32)
    m2 = jnp.asarray(_rope_matrix(), BF16)
    pos3 = positions.reshape(t // LANES, 1, LANES)
    lbp = hgrn_lower_bound.astype(F32).reshape(-1, B_HEADS, B_DIM).transpose(1, 0, 2)
    hnw = hgrn_norm[0].astype(F32).reshape(B_HEADS, 1, B_DIM)

    x0 = x.reshape(t, d)
    x1, w_in_t = _ffn(x0, row(ffn1_norm[0]), bf(ffn1_w_gate[0]), bf(ffn1_w_up[0]), bf(ffn1_w_down[0]),
                      ts["ffn_tm"], ts["ffn_tf"], w_next=w_in, tn=PROJ_TILE)
    later = [w_up_a, w_up_b, w_out, ffn2_w_gate, ffn2_w_up, ffn2_w_down, ple_w_gate, ple_w_proj]
    gates, out_a, out_b, (wa, wb, wo, wg2, wu2, wd2, wpg, wpp) = _mixers(
        x1, row(mix_norm[0]), w_in_t, pos3, invf8, m2, attn_sinks[0].astype(F32), lbp, hnw, later,
        ts["mix_tm"], seq)
    x2 = _merge(x1, out_a, out_b, gates, wa, wb, wo, ts["merge_tm"])
    x3 = _ffn(x2, row(ffn2_norm[0]), wg2, wu2, wd2, ts["ffn_tm"], ts["ffn_tf"])
    out = _ple(x3, p[0].reshape(t, -1), row(ple_norm[0]), row(final_norm), wpg, wpp, ts["ple_tm"], ts["ple_tn"])
    return out.reshape(batch, seq, d)
```

```python
import functools

import jax
import jax.numpy as jnp
import numpy as np
from jax import lax
from jax.experimental import pallas as pl
from jax.experimental.pallas import tpu as pltpu

EPS = 1e-6
A_HEADS = 16
A_KV_HEADS = 4
A_HEAD_DIM = 64
A_WIDTH = A_HEADS * A_HEAD_DIM
A_KV_WIDTH = A_KV_HEADS * A_HEAD_DIM
WINDOW = 128
ROT_DIM = A_HEAD_DIM // 4
ROT_HALF = ROT_DIM // 2
ROPE_THETA = 500000.0
B_HEADS = 8
B_DIM = 128
B_WIDTH = B_HEADS * B_DIM

LANES = 128
SUBLANES = 8
BF16_ROWS = 2 * SUBLANES
PROJ_TILE = 512
VMEM_LIMIT = 56 * 1024 * 1024
MIXER_VMEM_LIMIT = 58 * 1024 * 1024
LOG2E = 1.4426950408889634
Q_SCALE = A_HEAD_DIM ** -0.5 * LOG2E

F32 = jnp.float32
BF16 = jnp.bfloat16
NT_DIMS = (((1,), (1,)), ((), ()))
TN_DIMS = (((0,), (0,)), ((), ()))


def _params(*sem):
    return pltpu.CompilerParams(dimension_semantics=sem, vmem_limit_bytes=VMEM_LIMIT)


def _resident(shape):
    return pl.BlockSpec(shape, lambda *_: (0,) * len(shape), pipeline_mode=pl.Buffered(1))


def _rmsnorm(x, w):
    return x * lax.rsqrt(jnp.mean(x * x, axis=-1, keepdims=True) + EPS) * w


def _sigmoid(x):
    return 1.0 / (1.0 + jnp.exp(-x))


def _ffn_kernel(x_ref, nw_ref, wg_ref, wu_ref, wd_ref, *rest, tile_order=None):
    if len(rest) == 2:
        o_ref, h_ref = rest
    else:
        wn_ref, o_ref, wno_ref, h_ref = rest
        tn = wno_ref.shape[2]
        for k, jt in enumerate(tile_order):
            wno_ref[k] = wn_ref[:, jt * tn:(jt + 1) * tn].astype(wno_ref.dtype)

    @pl.when(pl.program_id(1) == 0)
    def _():
        x = x_ref[...]
        h_ref[...] = _rmsnorm(x, nw_ref[...]).astype(BF16)
        o_ref[...] = x

    h = h_ref[...]
    g = jnp.dot(h, wg_ref[...], preferred_element_type=F32)
    u = jnp.dot(h, wu_ref[...], preferred_element_type=F32)
    a = (0.5 * g * _sigmoid(g) * u).astype(BF16)
    o_ref[...] += jnp.dot(a, wd_ref[...], preferred_element_type=F32)


def _ffn(x, nw, wg, wu, wd, tm, tf, w_next=None, tn=None, tile_order=None):
    t, d = x.shape
    ff = wg.shape[1]
    nt, nf = t // tm, ff // tf
    in_specs = [
        pl.BlockSpec((tm, d), lambda i, f: (i, 0)),
        pl.BlockSpec((1, d), lambda i, f: (0, 0)),
        pl.BlockSpec((d, tf), lambda i, f: (0, f)),
        pl.BlockSpec((d, tf), lambda i, f: (0, f)),
        pl.BlockSpec((tf, d), lambda i, f: (f, 0)),
    ]
    out_specs = pl.BlockSpec((tm, d), lambda i, f: (i, 0))
    out_shape = jax.ShapeDtypeStruct((t, d), F32)
    args = (x, nw, wg, wu, wd)
    if w_next is not None:
        _, r, c = w_next.shape
        k = max(kk for kk in range(1, nf + 1) if r % (nt * kk) == 0 and (r // (nt * kk)) % BF16_ROWS == 0)
        rps = r // (nt * k)
        slab = lambda i, f: (0, i * k + jnp.minimum(f, k - 1), 0)
        in_specs.append(pl.BlockSpec((None, rps, c), slab))
        out_specs = [out_specs, pl.BlockSpec((c // tn, rps, tn), slab)]
        out_shape = [out_shape, jax.ShapeDtypeStruct((c // tn, r, tn), BF16)]
        args += (w_next,)
    return pl.pallas_call(
        functools.partial(_ffn_kernel, tile_order=tile_order),
        grid=(nt, nf),
        in_specs=in_specs,
        out_specs=out_specs,
        out_shape=out_shape,
        scratch_shapes=[pltpu.VMEM((tm, d), BF16)],
        compiler_params=_params("parallel", "arbitrary"),
        name="ffn",
    )(*args)


def _rope_matrix():
    m = np.zeros((2 * ROT_HALF, 3 * LANES), np.float32)
    for lane in range(LANES):
        dd = lane % A_HEAD_DIM
        if dd < ROT_DIM:
            m[dd % ROT_HALF, lane] = 1.0
        if dd < ROT_HALF:
            m[ROT_HALF + dd, LANES + lane] = -1.0
        elif dd < ROT_DIM:
            m[ROT_HALF + dd - ROT_HALF, 2 * LANES + lane] = 1.0
    return np.concatenate([m, m], axis=0)


def _rope_tables(pos_ref, invf_ref, m2_ref, tab_ref):
    for gi in range(tab_ref.shape[0] // LANES):
        ang = invf_ref[...] * pos_ref[gi].astype(F32)
        cs = jnp.concatenate([jnp.cos(ang) - 1.0, jnp.sin(ang)], axis=0)
        hi = cs.astype(BF16)
        lo = (cs - hi.astype(F32)).astype(BF16)
        tab_ref[gi * LANES:(gi + 1) * LANES, :] = lax.dot_general(
            jnp.concatenate([hi, lo], axis=0), m2_ref[...], TN_DIMS, preferred_element_type=F32)


def _rope_tile(r, tab_ref, kinds):
    cs = tab_ref[:, 0:LANES] + 1.0
    s1 = tab_ref[:, LANES:2 * LANES]
    s2 = tab_ref[:, 2 * LANES:3 * LANES]
    out = []
    for gi, kind in enumerate(kinds):
        rg = r[:, gi * LANES:(gi + 1) * LANES]
        if kind is not None:
            rg = rg * cs + pltpu.roll(rg, LANES - ROT_HALF, 1) * s1 + pltpu.roll(rg, ROT_HALF, 1) * s2
        if kind == "q":
            rg = rg * Q_SCALE
        out.append(rg)
    return jnp.concatenate(out, axis=1)


def _swa_block(sink_ref, q_at, kv, first, k_s, v_s, store):
    blk = kv.shape[0]
    low = lax.broadcasted_iota(jnp.int32, (1, LANES), 1) < A_HEAD_DIM
    qi = lax.broadcasted_iota(jnp.int32, (blk, 2 * blk), 0)
    kj = lax.broadcasted_iota(jnp.int32, (blk, 2 * blk), 1)
    allowed = (kj > qi) & (kj <= qi + blk) & (jnp.logical_not(first) | (kj >= blk))

    grp = A_HEADS // A_KV_HEADS
    for g in range(A_KV_HEADS):
        c0 = (g // 2) * LANES
        own = low if g % 2 == 0 else jnp.logical_not(low)
        kc = kv[:, c0:c0 + LANES].astype(F32)
        vc = kv[:, A_KV_WIDTH + c0:A_KV_WIDTH + c0 + LANES].astype(F32)
        k2 = jnp.where(own, kc, pltpu.roll(kc, A_HEAD_DIM, 1)).astype(BF16)
        v2 = jnp.where(own, vc, pltpu.roll(vc, A_HEAD_DIM, 1)).astype(BF16)
        k_s[g, blk:, :] = k2
        v_s[g, blk:, :] = v2
        qs = []
        for pr in range(grp // 2):
            qp = q_at((g * grp + 2 * pr) * A_HEAD_DIM).astype(F32)
            qs += [jnp.where(low, qp, 0.0), jnp.where(low, 0.0, qp)]
        qst = jnp.concatenate(qs, axis=0).astype(BF16)
        s = lax.dot_general(qst, k_s[g], NT_DIMS, preferred_element_type=F32)
        es, rden = [], []
        for r in range(grp):
            sk = sink_ref[g * grp + r] * LOG2E
            sr = jnp.where(allowed, s[r * blk:(r + 1) * blk], -jnp.inf)
            m = jnp.maximum(jnp.max(sr, axis=-1, keepdims=True), sk)
            e = jnp.exp2(sr - m)
            rden.append(1.0 / (jnp.sum(e, axis=-1, keepdims=True) + jnp.exp2(sk - m)))
            es.append(e.astype(BF16))
        o = jnp.dot(jnp.concatenate(es, axis=0), v_s[g], preferred_element_type=F32)
        for pr in range(grp // 2):
            oa = o[(2 * pr) * blk:(2 * pr + 1) * blk] * rden[2 * pr]
            ob = o[(2 * pr + 1) * blk:(2 * pr + 2) * blk] * rden[2 * pr + 1]
            store((g * grp + 2 * pr) * A_HEAD_DIM, jnp.where(low, oa, ob))
        k_s[g, :blk, :] = k2
        v_s[g, :blk, :] = v2


HGRN_CHUNK = 256
HGRN_QUAD = 128


def _hgrn_chunk(a, nw, qp, fp, v, og, st, c_s, kc_s, v_s):
    cn = qp.shape[0]
    nblk = cn // SUBLANES
    quad = min(HGRN_QUAD, cn)
    bpq = quad // SUBLANES
    assert cn in (quad, 2 * quad)

    ea = jnp.exp(a - jnp.max(a, axis=0, keepdims=True))
    lb = ea[0:1] / jnp.sum(ea, axis=0, keepdims=True)

    q = qp * _sigmoid(qp)
    fg = lb + (1.0 - lb) * _sigmoid(fp)
    v_s[...] = v

    row8 = lax.broadcasted_iota(jnp.int32, (cn, LANES), 0) % SUBLANES
    w = jnp.log(fg) * LOG2E
    for sh in (1, 2, 4):
        w = w + jnp.where(row8 >= sh, pltpu.roll(w, sh, 0), 0.0)
    c_s[...] = w
    run = jnp.zeros((SUBLANES, LANES), F32)
    cb = []
    for m in range(nblk):
        cb.append(w[m * SUBLANES:(m + 1) * SUBLANES] + run)
        run = run + c_s[pl.ds(m * SUBLANES + SUBLANES - 1, 1), :]
    c = jnp.concatenate(cb, axis=0)
    c_s[...] = c
    kc = c - jnp.log(1.0 - fg) * LOG2E
    kc_s[...] = kc
    qb = [q[m * SUBLANES:(m + 1) * SUBLANES] for m in range(nblk)]
    kcb = [kc[m * SUBLANES:(m + 1) * SUBLANES] for m in range(nblk)]
    zero = jnp.zeros((SUBLANES, LANES), F32)

    def level(blocks_q, blocks_k, b):
        qt, kt = [], []
        for m in blocks_q:
            anchor = (m * SUBLANES) // (2 * b) * (2 * b) + b
            if m * SUBLANES >= anchor:
                qt.append(qb[m] * jnp.exp2(cb[m] - c_s[pl.ds(anchor - 1, 1), :]))
            else:
                qt.append(zero)
        for m in blocks_k:
            anchor = (m * SUBLANES) // (2 * b) * (2 * b) + b
            if m * SUBLANES < anchor:
                kt.append(jnp.exp2(c_s[pl.ds(anchor - 1, 1), :] - kcb[m]))
            else:
                kt.append(zero)
        return lax.dot_general(jnp.concatenate(qt, axis=0).astype(BF16), jnp.concatenate(kt, axis=0).astype(BF16),
                               NT_DIMS, preferred_element_type=F32)

    blkxor = (lax.broadcasted_iota(jnp.int32, (quad, quad), 0) ^ lax.broadcasted_iota(jnp.int32, (quad, quad), 1))
    sq = []
    for qd in range(cn // quad):
        blocks = list(range(qd * bpq, (qd + 1) * bpq))
        bs = []
        b = SUBLANES
        while b < quad:
            bs.append(b)
            b *= 2
        sc = level(blocks, blocks, bs[-1])
        for b in reversed(bs[:-1]):
            sc = jnp.where(blkxor < 2 * b, level(blocks, blocks, b), sc)
        sq.append(sc.astype(BF16))

    vb = v.astype(BF16)
    if cn == quad:
        o = jnp.dot(sq[0], vb, preferred_element_type=F32)
    else:
        lo_blocks, hi_blocks = list(range(bpq)), list(range(bpq, nblk))
        cross = level(hi_blocks, lo_blocks, quad).astype(BF16)
        o = jnp.concatenate([
            jnp.dot(sq[0], vb[:quad], preferred_element_type=F32),
            jnp.dot(jnp.concatenate([cross, sq[1]], axis=1), vb, preferred_element_type=F32)], axis=0)

    r8 = lax.broadcasted_iota(jnp.int32, (SUBLANES, LANES), 0)
    od = [zero] * nblk
    for j in range(SUBLANES):
        for m in range(nblk):
            row = pl.ds(m * SUBLANES + j, 1)
            dec = jnp.exp2(jnp.where(r8 >= j, cb[m] - kc_s[row, :], -jnp.inf))
            od[m] = od[m] + jnp.sum(qb[m] * dec, axis=-1, keepdims=True) * v_s[row, :]

    o = (o + jnp.concatenate(od, axis=0)
         + lax.dot_general((q * jnp.exp2(c)).astype(BF16), st.astype(BF16), NT_DIMS, preferred_element_type=F32))

    clast = c[cn - 1:cn]
    kl = jnp.exp2(clast - kc).astype(BF16)
    st_next = jnp.exp2(clast) * st + lax.dot_general(vb, kl, TN_DIMS, preferred_element_type=F32)
    return _rmsnorm(o, nw) * (og * _sigmoid(og)), st_next


def _slab_blocks(r, c, nblocks):
    for ncb in range(1, nblocks + 1):
        if nblocks % ncb or c % ncb or r % (nblocks // ncb):
            continue
        br, bc = r // (nblocks // ncb), c // ncb
        if br % BF16_ROWS == 0 and bc % LANES == 0:
            return br, bc, ncb
    return None


def _cast_specs(weights, nt, nsteps):
    ins, outs = [], []
    for wt in weights:
        _, r, c = wt.shape
        k, (br, bc, ncb) = next((kk, _slab_blocks(r, c, nt * kk)) for kk in range(nsteps, 0, -1)
                                if _slab_blocks(r, c, nt * kk))
        blk = functools.partial(lambda i, j, k, ncb: divmod(i * k + jnp.minimum(j, k - 1), ncb), k=k, ncb=ncb)
        ins.append(pl.BlockSpec((None, br, bc), functools.partial(lambda i, j, blk: (0, *blk(i, j)), blk=blk)))
        outs.append(pl.BlockSpec((br, bc), blk))
    return ins, outs


def _mixer_kernel(sink_ref, pos_ref, invf_ref, m2_ref, x_ref, nw_ref, w_ref, lbp_ref, hnw_ref, *rest,
                  plan, n_hin, ncast, nb_seq, nt_seq, u_early, u_gate):
    cast_in, rest = rest[:ncast], rest[ncast:]
    g_ref, oa_ref, ob_ref = rest[:3]
    cast_out, rest = rest[3:3 + ncast], rest[3 + ncast:]
    h_ref, tab_ref, qkv_s, hin_s, k_s, v_s, st_ref, c_s, kc_s, v2_s = rest
    i, j = pl.program_id(0), pl.program_id(1)
    tm = x_ref.shape[0]
    n_rope = len(plan)
    cn = c_s.shape[1]

    for w_in_ref, w_out_ref in zip(cast_in, cast_out):
        w_out_ref[...] = w_in_ref[...].astype(w_out_ref.dtype)

    @pl.when((i == 0) & (j == 0))
    def _():
        k_s[...] = jnp.zeros_like(k_s)
        v_s[...] = jnp.zeros_like(v_s)
        st_ref[...] = jnp.zeros_like(st_ref)

    @pl.when(j == 0)
    def _():
        h_ref[...] = _rmsnorm(x_ref[...], nw_ref[...]).astype(BF16)
        _rope_tables(pos_ref, invf_ref, m2_ref, tab_ref)

    def project():
        return jnp.dot(h_ref[...], w_ref[...], preferred_element_type=F32)

    for jj, kinds in enumerate(plan):
        @pl.when(j == jj)
        def _(jj=jj, kinds=kinds):
            qkv_s[jj] = _rope_tile(project(), tab_ref, kinds).astype(BF16)

    hpt = PROJ_TILE // B_DIM
    nchunk = tm // cn
    fresh = i % nt_seq == 0

    def stash_and_attend(u):
        r = project().astype(BF16)
        for hh in range(hpt):
            hin_s[(u % 4) * B_HEADS + (u // 4) * hpt + hh] = r[:, hh * B_DIM:(hh + 1) * B_DIM]
        rows = pl.ds(pl.multiple_of(u * WINDOW, WINDOW), WINDOW)
        first = (i * n_hin + u) % nb_seq == 0
        q_at = lambda col: qkv_s[col // PROJ_TILE, rows, col % PROJ_TILE:col % PROJ_TILE + LANES]

        def store(col, val):
            oa_ref[rows, col:col + LANES] = val.astype(oa_ref.dtype)

        _swa_block(sink_ref, q_at, qkv_s[n_rope - 1, rows, :], first, k_s, v_s, store)

    def hgrn_units(base, count):
        ids = [base + e for e in range(count)]
        heads = [idx // (hpt * nchunk) * hpt + idx % hpt for idx in ids]
        chunks = [idx % (hpt * nchunk) // hpt for idx in ids]
        sts = [jnp.where(fresh & (ck == 0), 0.0, st_ref[hd]) for hd, ck in zip(heads, chunks)]
        for e, (hd, ck) in enumerate(zip(heads, chunks)):
            rows = pl.ds(pl.multiple_of(ck * cn, cn), cn)
            part = lambda pi: hin_s[pi * B_HEADS + hd, rows, :].astype(F32)
            o, sts[e] = _hgrn_chunk(lbp_ref[hd], hnw_ref[hd], part(0), part(1), part(2), part(3), sts[e],
                                    c_s.at[e], kc_s.at[e], v2_s.at[e])
            ob_ref[hd, rows, :] = o.astype(ob_ref.dtype)
        for hd, st in zip(heads, sts):
            st_ref[hd] = st

    @pl.when((j >= n_rope) & (j < n_rope + 4))
    def _():
        stash_and_attend(j - n_rope)

    @pl.when((j >= n_rope + 4) & (j < n_rope + n_hin))
    def _():
        stash_and_attend(j - n_rope)
        hgrn_units((j - n_rope - 4) * u_early, u_early)

    @pl.when(j >= n_rope + n_hin)
    def _():
        g_ref[...] = project().astype(g_ref.dtype)
        hgrn_units((n_hin - 4) * u_early + (j - n_rope - n_hin) * u_gate, u_gate)


def _mixers(x, nw, w, pos3, invf8, m2, sinks, lbp, hnw, weights, tm, seq):
    t, d = x.shape
    ntile, _, tn = w.shape
    assert tn == PROJ_TILE and A_WIDTH % tn == 0 and 2 * A_KV_WIDTH == tn and seq % tm == 0
    kind = lambda col: "q" if col < A_WIDTH else ("k" if col < A_WIDTH + A_KV_WIDTH else None)
    plan = [tuple(kind(jj * tn + gi * LANES) for gi in range(tn // LANES)) for jj in range(ntile)]
    plan = plan[:max(jj + 1 for jj, kinds in enumerate(plan) if any(kinds))]
    n_rope = len(plan)
    n_hin = 4 * B_WIDTH // tn
    n_gate = ntile - n_rope - n_hin
    cn = min(HGRN_CHUNK, tm)
    nt = t // tm
    u_early = 0
    units = B_HEADS * (tm // cn)
    u_gate = (units - (n_hin - 4) * u_early) // n_gate
    assert tm // WINDOW == n_hin and n_hin == 4 * B_WIDTH // tn == 8 and tn // B_DIM == 4
    assert (n_hin - 4) * u_early + n_gate * u_gate == units and (n_hin - 4) * u_early <= units // 2
    assert max(u_early, u_gate) <= tn // B_DIM
    cast_in, cast_out = _cast_specs(weights, nt, ntile)
    const2 = lambda i, j: (0, 0)
    const3 = lambda i, j: (0, 0, 0)
    outs = pl.pallas_call(
        functools.partial(_mixer_kernel, plan=plan, n_hin=n_hin, ncast=len(weights),
                          nb_seq=seq // WINDOW, nt_seq=seq // tm, u_early=u_early, u_gate=u_gate),
        grid=(nt, ntile),
        in_specs=[
            pl.BlockSpec(memory_space=pltpu.SMEM),
            pl.BlockSpec((tm // LANES, 1, LANES), lambda i, j: (i, 0, 0)),
            pl.BlockSpec(invf8.shape, const2),
            pl.BlockSpec(m2.shape, const2),
            pl.BlockSpec((tm, d), lambda i, j: (i, 0)),
            pl.BlockSpec((1, d), const2),
            pl.BlockSpec((None, d, tn), lambda i, j: (j, 0, 0)),
            pl.BlockSpec(lbp.shape, const3),
            pl.BlockSpec(hnw.shape, const3),
            *cast_in,
        ],
        out_specs=[
            pl.BlockSpec((None, tm, tn), lambda i, j: (jnp.maximum(j - n_rope - n_hin, 0), i, 0)),
            pl.BlockSpec((tm, A_WIDTH), lambda i, j: (i, 0)),
            pl.BlockSpec((B_HEADS, tm, B_DIM), lambda i, j: (0, i, 0)),
            *cast_out,
        ],
        out_shape=[
            jax.ShapeDtypeStruct((n_gate, t, tn), BF16),
            jax.ShapeDtypeStruct((t, A_WIDTH), BF16),
            jax.ShapeDtypeStruct((B_HEADS, t, B_DIM), BF16),
            *[jax.ShapeDtypeStruct(wt.shape[1:], BF16) for wt in weights],
        ],
        scratch_shapes=[
            pltpu.VMEM((tm, d), BF16),
            pltpu.VMEM((tm, 3 * LANES), F32),
            pltpu.VMEM((n_rope, tm, tn), BF16),
            pltpu.VMEM((4 * B_HEADS, tm, B_DIM), BF16),
            pltpu.VMEM((A_KV_HEADS, 2 * WINDOW, LANES), BF16),
            pltpu.VMEM((A_KV_HEADS, 2 * WINDOW, LANES), BF16),
            pltpu.VMEM((B_HEADS, B_DIM, B_DIM), F32),
            pltpu.VMEM((max(u_early, u_gate), cn, B_DIM), F32),
            pltpu.VMEM((max(u_early, u_gate), cn, B_DIM), F32),
            pltpu.VMEM((max(u_early, u_gate), cn, B_DIM), F32),
        ],
        compiler_params=pltpu.CompilerParams(dimension_semantics=("arbitrary", "arbitrary"),
                                             vmem_limit_bytes=MIXER_VMEM_LIMIT),
        name="mixers",
    )(sinks, pos3, invf8, m2, x, nw, w, lbp, hnw, *weights)
    return outs[0], outs[1], outs[2], list(outs[3:])


def _merge_kernel(*refs, nchunk):
    x_ref, oa_ref, ob_ref = refs[:3]
    ga_refs, gb_refs = refs[3:3 + nchunk], refs[3 + nchunk:3 + 2 * nchunk]
    wa_ref, wb_ref, wo_ref, o_ref = refs[3 + 2 * nchunk:]
    tn = ga_refs[0].shape[1]
    o_ref[...] = x_ref[...]
    oa = oa_ref[...]
    ob = jnp.concatenate([ob_ref[hd] for hd in range(ob_ref.shape[0])], axis=1)
    for c in range(nchunk):
        sl = slice(c * tn, (c + 1) * tn)
        ma = jnp.dot(oa, wa_ref[:, sl], preferred_element_type=F32)
        mb = jnp.dot(ob, wb_ref[:, sl], preferred_element_type=F32)
        merged = _sigmoid(ga_refs[c][...].astype(F32)) * ma + _sigmoid(gb_refs[c][...].astype(F32)) * mb
        o_ref[...] += jnp.dot(merged.astype(BF16), wo_ref[sl, :], preferred_element_type=F32)


def _merge(x, out_a, out_b, gates, wa, wb, wo, tm):
    t, d = x.shape
    tn = PROJ_TILE
    nchunk = d // tn
    gate = lambda c0: [pl.BlockSpec((None, tm, tn), functools.partial(lambda i, cc: (cc, i, 0), cc=c0 + c))
                       for c in range(nchunk)]
    return pl.pallas_call(
        functools.partial(_merge_kernel, nchunk=nchunk),
        grid=(t // tm,),
        in_specs=[
            pl.BlockSpec((tm, d), lambda i: (i, 0)),
            pl.BlockSpec((tm, A_WIDTH), lambda i: (i, 0)),
            pl.BlockSpec((B_HEADS, tm, B_DIM), lambda i: (0, i, 0)),
            *gate(0), *gate(nchunk),
            _resident(wa.shape), _resident(wb.shape), _resident(wo.shape),
        ],
        out_specs=pl.BlockSpec((tm, d), lambda i: (i, 0)),
        out_shape=jax.ShapeDtypeStruct((t, d), F32),
        compiler_params=_params("parallel"),
        name="merge",
    )(x, out_a, out_b, *([gates] * (2 * nchunk)), wa, wb, wo)


PLE_SUBTILES = 2


def _ple_kernel(x_ref, p_ref, nw_ref, fw_ref, wg_ref, wp_ref, o_ref, *, tn):
    d = x_ref.shape[1]
    rs = x_ref.shape[0] // PLE_SUBTILES
    for s in range(PLE_SUBTILES):
        rows = slice(s * rs, (s + 1) * rs)
        x = x_ref[rows, :]
        h = _rmsnorm(x, nw_ref[...]).astype(BF16)
        pb = p_ref[rows, :].astype(BF16)
        ss = jnp.zeros((rs, 1), F32)
        for c in range(d // tn):
            sl = slice(c * tn, (c + 1) * tn)
            g = _sigmoid(jnp.dot(h, wg_ref[:, sl], preferred_element_type=F32))
            y = x[:, sl] + g * jnp.dot(pb, wp_ref[:, sl], preferred_element_type=F32)
            ss = ss + jnp.sum(y * y, axis=-1, keepdims=True)
            o_ref[rows, sl] = y
        o_ref[rows, :] = o_ref[rows, :] * lax.rsqrt(ss * (1.0 / d) + EPS) * fw_ref[...]


def _ple(x, p, nw, fw, wg, wp, tm, tn):
    t, d = x.shape
    pd = p.shape[1]
    return pl.pallas_call(
        functools.partial(_ple_kernel, tn=tn),
        grid=(t // tm,),
        in_specs=[
            pl.BlockSpec((tm, d), lambda i: (i, 0)),
            pl.BlockSpec((tm, pd), lambda i: (i, 0)),
            pl.BlockSpec((1, d), lambda i: (0, 0)),
            pl.BlockSpec((1, d), lambda i: (0, 0)),
            _resident(wg.shape), _resident(wp.shape),
        ],
        out_specs=pl.BlockSpec((tm, d), lambda i: (i, 0)),
        out_shape=jax.ShapeDtypeStruct((t, d), F32),
        compiler_params=_params("parallel"),
        name="ple",
    )(x, p, nw, fw, wg, wp)


def _tiles(t, d, ff):
    return dict(ffn_tm=min(1024, t), ffn_tf=min(512, ff), mix_tm=min(1024, t),
                merge_tm=min(512, t), ple_tm=min(512, t), ple_tn=min(512, d))


def kernel(x, p, positions, ffn1_norm, ffn1_w_gate, ffn1_w_up, ffn1_w_down, mix_norm, w_in, attn_sinks, hgrn_lower_bound, hgrn_norm, w_up_a, w_up_b, w_out, ffn2_norm, ffn2_w_gate, ffn2_w_up, ffn2_w_down, ple_norm, ple_w_gate, ple_w_proj, final_norm):
    batch, seq, d = x.shape
    assert ffn1_norm.shape[0] == 1, "single-layer stack"
    t = batch * seq
    ts = _tiles(t, d, ffn1_w_gate.shape[-1])
    bf = lambda wt: wt.astype(BF16)
    row = lambda wt: wt.reshape(1, -1).astype(F32)

    inv_freq = jnp.power(jnp.float32(ROPE_THETA), -jnp.arange(0, ROT_DIM, 2, dtype=F32) / ROT_DIM)
    invf8 = jnp.broadcast_to(inv_freq[:, None], (ROT_HALF, LANES)).astype(F32)
    m2 = jnp.asarray(_rope_matrix(), BF16)
    pos3 = positions.reshape(t // LANES, 1, LANES)
    lbp = hgrn_lower_bound.astype(F32).reshape(-1, B_HEADS, B_DIM).transpose(1, 0, 2)
    hnw = hgrn_norm[0].astype(F32).reshape(B_HEADS, 1, B_DIM)

    x0 = x.reshape(t, d)
    n_rope = (A_WIDTH + 2 * A_KV_WIDTH) // PROJ_TILE
    halves = B_WIDTH // PROJ_TILE
    tile_order = (list(range(n_rope)) + [n_rope + part * halves + half for half in range(halves) for part in range(4)]
                  + list(range(n_rope + 4 * halves, w_in.shape[-1] // PROJ_TILE)))
    x1, w_in_t = _ffn(x0, row(ffn1_norm[0]), bf(ffn1_w_gate[0]), bf(ffn1_w_up[0]), bf(ffn1_w_down[0]),
                      ts["ffn_tm"], ts["ffn_tf"], w_next=w_in, tn=PROJ_TILE, tile_order=tile_order)
    later = [ffn2_w_gate, ffn2_w_up, ffn2_w_down]
    gates, out_a, out_b, (wg2, wu2, wd2) = _mixers(
        x1, row(mix_norm[0]), w_in_t, pos3, invf8, m2, attn_sinks[0].astype(F32), lbp, hnw, later,
        ts["mix_tm"], seq)
    x2 = _merge(x1, out_a, out_b, gates, bf(w_up_a[0]), bf(w_up_b[0]), bf(w_out[0]), ts["merge_tm"])
    x3 = _ffn(x2, row(ffn2_norm[0]), wg2, wu2, wd2, ts["ffn_tm"], ts["ffn_tf"])
    out = _ple(x3, p[0].reshape(t, -1), row(ple_norm[0]), row(final_norm), bf(ple_w_gate[0]), bf(ple_w_proj[0]),
               ts["ple_tm"], ts["ple_tn"])
    return out.reshape(batch, seq, d)
```

```python
import functools

import jax
import jax.numpy as jnp
import numpy as np
from jax import lax
from jax.experimental import pallas as pl
from jax.experimental.pallas import tpu as pltpu

EPS = 1e-6
A_HEADS = 16
A_KV_HEADS = 4
A_HEAD_DIM = 64
A_WIDTH = A_HEADS * A_HEAD_DIM
A_KV_WIDTH = A_KV_HEADS * A_HEAD_DIM
WINDOW = 128
ROT_DIM = A_HEAD_DIM // 4
ROT_HALF = ROT_DIM // 2
ROPE_THETA = 500000.0
B_HEADS = 8
B_DIM = 128
B_WIDTH = B_HEADS * B_DIM

LANES = 128
SUBLANES = 8
BF16_ROWS = 2 * SUBLANES
PROJ_TILE = 512
VMEM_LIMIT = 56 * 1024 * 1024
MIXER_VMEM_LIMIT = 58 * 1024 * 1024
LOG2E = 1.4426950408889634
Q_SCALE = A_HEAD_DIM ** -0.5 * LOG2E

F32 = jnp.float32
BF16 = jnp.bfloat16
NT_DIMS = (((1,), (1,)), ((), ()))
TN_DIMS = (((0,), (0,)), ((), ()))


def _params(*sem, vmem=VMEM_LIMIT):
    return pltpu.CompilerParams(dimension_semantics=sem, vmem_limit_bytes=vmem)


def _resident(shape):
    return pl.BlockSpec(shape, lambda *_: (0,) * len(shape), pipeline_mode=pl.Buffered(1))


def _rmsnorm(x, w):
    return x * lax.rsqrt(jnp.mean(x * x, axis=-1, keepdims=True) + EPS) * w


def _sigmoid(x):
    return 1.0 / (1.0 + jnp.exp(-x))


def _ffn_step(x_ref, nw_ref, wg, wu, wd, o_ref, h_ref, active=None):
    first = pl.program_id(1) == 0
    on = (lambda c: c) if active is None else (lambda c: c & active)

    @pl.when(on(first))
    def _():
        x = x_ref[...]
        h_ref[...] = _rmsnorm(x, nw_ref[...]).astype(BF16)
        o_ref[...] = x

    def body():
        h = h_ref[...]
        g = jnp.dot(h, wg[...], preferred_element_type=F32)
        u = jnp.dot(h, wu[...], preferred_element_type=F32)
        a = (0.5 * g * _sigmoid(g) * u).astype(BF16)
        o_ref[...] += jnp.dot(a, wd[...], preferred_element_type=F32)

    if active is None:
        body()
    else:
        pl.when(active)(body)


def _ffn_first_kernel(x_ref, nw_ref, wg_ref, wu_ref, wd_ref, o_ref, wgo_ref, wuo_ref, wdo_ref, h_ref):
    wg, wu, wd = wg_ref[...].astype(BF16), wu_ref[...].astype(BF16), wd_ref[...].astype(BF16)
    wgo_ref[...] = wg
    wuo_ref[...] = wu
    wdo_ref[...] = wd
    _ffn_step(x_ref, nw_ref, wg, wu, wd, o_ref, h_ref)


def _ffn_first(x, nw, wg, wu, wd, tm, tf):
    t, d = x.shape
    ff = wg.shape[2]
    return pl.pallas_call(
        _ffn_first_kernel,
        grid=(1, ff // tf),
        in_specs=[
            pl.BlockSpec((tm, d), lambda i, f: (0, 0), pipeline_mode=pl.Buffered(1)),
            pl.BlockSpec((1, d), lambda i, f: (0, 0)),
            pl.BlockSpec((None, d, tf), lambda i, f: (0, 0, f)),
            pl.BlockSpec((None, d, tf), lambda i, f: (0, 0, f)),
            pl.BlockSpec((None, tf, d), lambda i, f: (0, f, 0)),
        ],
        out_specs=[
            pl.BlockSpec((tm, d), lambda i, f: (0, 0)),
            pl.BlockSpec((d, tf), lambda i, f: (0, f)),
            pl.BlockSpec((d, tf), lambda i, f: (0, f)),
            pl.BlockSpec((tf, d), lambda i, f: (f, 0)),
        ],
        out_shape=[jax.ShapeDtypeStruct((tm, d), F32), jax.ShapeDtypeStruct((d, ff), BF16),
                   jax.ShapeDtypeStruct((d, ff), BF16), jax.ShapeDtypeStruct((ff, d), BF16)],
        scratch_shapes=[pltpu.VMEM((tm, d), BF16)],
        compiler_params=_params("arbitrary", "arbitrary"),
        name="ffn_first",
    )(x, nw, wg, wu, wd)


def _ffn_kernel(x_ref, nw_ref, wg_ref, wu_ref, wd_ref, *rest, tile_order=None, has_first=False):
    rest = list(rest)
    first_ref = rest.pop(0) if has_first else None
    wn_ref = rest.pop(0) if tile_order is not None else None
    o_ref = rest.pop(0)
    wno_ref = rest.pop(0) if tile_order is not None else None
    h_ref = rest.pop(0)
    if wn_ref is not None:
        tn = wno_ref.shape[2]
        for k, jt in enumerate(tile_order):
            wno_ref[k] = wn_ref[:, jt * tn:(jt + 1) * tn].astype(wno_ref.dtype)

    if not has_first:
        _ffn_step(x_ref, nw_ref, wg_ref, wu_ref, wd_ref, o_ref, h_ref)
        return

    (sem,) = rest
    i, f = pl.program_id(0), pl.program_id(1)

    @pl.when((i == 0) & (f == 0))
    def _():
        cp = pltpu.make_async_copy(first_ref, o_ref, sem)
        cp.start()
        cp.wait()

    _ffn_step(x_ref, nw_ref, wg_ref, wu_ref, wd_ref, o_ref, h_ref, active=i > 0)


def _ffn(x, nw, wg, wu, wd, tm, tf, w_next=None, tn=None, tile_order=None, first=None):
    t, d = x.shape
    ff = wg.shape[1]
    nt, nf = t // tm, ff // tf
    ftile = (lambda i, f: f) if first is None else (lambda i, f: jnp.where(i == 0, 0, f))
    xtile = (lambda i: i) if first is None else (lambda i: jnp.maximum(i, 1))
    in_specs = [
        pl.BlockSpec((tm, d), lambda i, f: (xtile(i), 0)),
        pl.BlockSpec((1, d), lambda i, f: (0, 0)),
        pl.BlockSpec((d, tf), lambda i, f: (0, ftile(i, f))),
        pl.BlockSpec((d, tf), lambda i, f: (0, ftile(i, f))),
        pl.BlockSpec((tf, d), lambda i, f: (ftile(i, f), 0)),
    ]
    out_specs = pl.BlockSpec((tm, d), lambda i, f: (i, 0))
    out_shape = jax.ShapeDtypeStruct((t, d), F32)
    args = (x, nw, wg, wu, wd)
    scratch = [pltpu.VMEM((tm, d), BF16)]
    if first is not None:
        in_specs.append(pl.BlockSpec(memory_space=pl.ANY))
        args += (first,)
        scratch.append(pltpu.SemaphoreType.DMA(()))
    if w_next is not None:
        _, r, c = w_next.shape
        k = max(kk for kk in range(1, nf + 1) if r % (nt * kk) == 0 and (r // (nt * kk)) % BF16_ROWS == 0)
        rps = r // (nt * k)
        slab = lambda i, f: (0, i * k + jnp.minimum(f, k - 1), 0)
        in_specs.append(pl.BlockSpec((None, rps, c), slab))
        out_specs = [out_specs, pl.BlockSpec((c // tn, rps, tn), slab)]
        out_shape = [out_shape, jax.ShapeDtypeStruct((c // tn, r, tn), BF16)]
        args += (w_next,)
    return pl.pallas_call(
        functools.partial(_ffn_kernel, tile_order=tile_order, has_first=first is not None),
        grid=(nt, nf),
        in_specs=in_specs,
        out_specs=out_specs,
        out_shape=out_shape,
        scratch_shapes=scratch,
        compiler_params=_params("arbitrary", "arbitrary", vmem=MIXER_VMEM_LIMIT),
        name="ffn",
    )(*args)


def _rope_matrix():
    m = np.zeros((2 * ROT_HALF, 3 * LANES), np.float32)
    for lane in range(LANES):
        dd = lane % A_HEAD_DIM
        if dd < ROT_DIM:
            m[dd % ROT_HALF, lane] = 1.0
        if dd < ROT_HALF:
            m[ROT_HALF + dd, LANES + lane] = -1.0
        elif dd < ROT_DIM:
            m[ROT_HALF + dd - ROT_HALF, 2 * LANES + lane] = 1.0
    return np.concatenate([m, m], axis=0)


def _rope_tables(pos_ref, invf_ref, m2_ref, tab_ref):
    for gi in range(tab_ref.shape[0] // LANES):
        ang = invf_ref[...] * pos_ref[gi].astype(F32)
        cs = jnp.concatenate([jnp.cos(ang) - 1.0, jnp.sin(ang)], axis=0)
        hi = cs.astype(BF16)
        lo = (cs - hi.astype(F32)).astype(BF16)
        tab_ref[gi * LANES:(gi + 1) * LANES, :] = lax.dot_general(
            jnp.concatenate([hi, lo], axis=0), m2_ref[...], TN_DIMS, preferred_element_type=F32)


def _rope_tile(r, tab_ref, kinds):
    cs = tab_ref[:, 0:LANES] + 1.0
    s1 = tab_ref[:, LANES:2 * LANES]
    s2 = tab_ref[:, 2 * LANES:3 * LANES]
    out = []
    for gi, kind in enumerate(kinds):
        rg = r[:, gi * LANES:(gi + 1) * LANES]
        if kind is not None:
            rg = rg * cs + pltpu.roll(rg, LANES - ROT_HALF, 1) * s1 + pltpu.roll(rg, ROT_HALF, 1) * s2
        if kind == "q":
            rg = rg * Q_SCALE
        out.append(rg)
    return jnp.concatenate(out, axis=1)


def _swa_block(sink_ref, q_at, kv, first, k_s, v_s, store):
    blk = kv.shape[0]
    low = lax.broadcasted_iota(jnp.int32, (1, LANES), 1) < A_HEAD_DIM
    qi = lax.broadcasted_iota(jnp.int32, (blk, 2 * blk), 0)
    kj = lax.broadcasted_iota(jnp.int32, (blk, 2 * blk), 1)
    allowed = (kj > qi) & (kj <= qi + blk) & (jnp.logical_not(first) | (kj >= blk))

    grp = A_HEADS // A_KV_HEADS
    for g in range(A_KV_HEADS):
        c0 = (g // 2) * LANES
        own = low if g % 2 == 0 else jnp.logical_not(low)
        kc = kv[:, c0:c0 + LANES].astype(F32)
        vc = kv[:, A_KV_WIDTH + c0:A_KV_WIDTH + c0 + LANES].astype(F32)
        k2 = jnp.where(own, kc, pltpu.roll(kc, A_HEAD_DIM, 1)).astype(BF16)
        v2 = jnp.where(own, vc, pltpu.roll(vc, A_HEAD_DIM, 1)).astype(BF16)
        k_s[g, blk:, :] = k2
        v_s[g, blk:, :] = v2
        qs = []
        for pr in range(grp // 2):
            qp = q_at((g * grp + 2 * pr) * A_HEAD_DIM).astype(F32)
            qs += [jnp.where(low, qp, 0.0), jnp.where(low, 0.0, qp)]
        qst = jnp.concatenate(qs, axis=0).astype(BF16)
        s = lax.dot_general(qst, k_s[g], NT_DIMS, preferred_element_type=F32)
        es, rden = [], []
        for r in range(grp):
            sk = sink_ref[g * grp + r] * LOG2E
            sr = jnp.where(allowed, s[r * blk:(r + 1) * blk], -jnp.inf)
            m = jnp.maximum(jnp.max(sr, axis=-1, keepdims=True), sk)
            e = jnp.exp2(sr - m)
            rden.append(1.0 / (jnp.sum(e, axis=-1, keepdims=True) + jnp.exp2(sk - m)))
            es.append(e.astype(BF16))
        o = jnp.dot(jnp.concatenate(es, axis=0), v_s[g], preferred_element_type=F32)
        for pr in range(grp // 2):
            oa = o[(2 * pr) * blk:(2 * pr + 1) * blk] * rden[2 * pr]
            ob = o[(2 * pr + 1) * blk:(2 * pr + 2) * blk] * rden[2 * pr + 1]
            store((g * grp + 2 * pr) * A_HEAD_DIM, jnp.where(low, oa, ob))
        k_s[g, :blk, :] = k2
        v_s[g, :blk, :] = v2


HGRN_CHUNK = 256
HGRN_QUAD = 128


def _hgrn_chunk(a, nw, qp, fp, v, og, st, c_s, kc_s, v_s):
    cn = qp.shape[0]
    nblk = cn // SUBLANES
    quad = min(HGRN_QUAD, cn)
    bpq = quad // SUBLANES
    assert cn in (quad, 2 * quad)

    ea = jnp.exp(a - jnp.max(a, axis=0, keepdims=True))
    lb = ea[0:1] / jnp.sum(ea, axis=0, keepdims=True)

    q = qp * _sigmoid(qp)
    fg = lb + (1.0 - lb) * _sigmoid(fp)
    v_s[...] = v

    row8 = lax.broadcasted_iota(jnp.int32, (cn, LANES), 0) % SUBLANES
    w = jnp.log(fg) * LOG2E
    for sh in (1, 2, 4):
        w = w + jnp.where(row8 >= sh, pltpu.roll(w, sh, 0), 0.0)
    c_s[...] = w
    run = jnp.zeros((SUBLANES, LANES), F32)
    cb = []
    for m in range(nblk):
        cb.append(w[m * SUBLANES:(m + 1) * SUBLANES] + run)
        run = run + c_s[pl.ds(m * SUBLANES + SUBLANES - 1, 1), :]
    c = jnp.concatenate(cb, axis=0)
    c_s[...] = c
    kc = c - jnp.log(1.0 - fg) * LOG2E
    kc_s[...] = kc
    qb = [q[m * SUBLANES:(m + 1) * SUBLANES] for m in range(nblk)]
    kcb = [kc[m * SUBLANES:(m + 1) * SUBLANES] for m in range(nblk)]
    zero = jnp.zeros((SUBLANES, LANES), F32)

    def level(blocks_q, blocks_k, b):
        qt, kt = [], []
        for m in blocks_q:
            anchor = (m * SUBLANES) // (2 * b) * (2 * b) + b
            if m * SUBLANES >= anchor:
                qt.append(qb[m] * jnp.exp2(cb[m] - c_s[pl.ds(anchor - 1, 1), :]))
            else:
                qt.append(zero)
        for m in blocks_k:
            anchor = (m * SUBLANES) // (2 * b) * (2 * b) + b
            if m * SUBLANES < anchor:
                kt.append(jnp.exp2(c_s[pl.ds(anchor - 1, 1), :] - kcb[m]))
            else:
                kt.append(zero)
        return lax.dot_general(jnp.concatenate(qt, axis=0).astype(BF16), jnp.concatenate(kt, axis=0).astype(BF16),
                               NT_DIMS, preferred_element_type=F32)

    blkxor = (lax.broadcasted_iota(jnp.int32, (quad, quad), 0) ^ lax.broadcasted_iota(jnp.int32, (quad, quad), 1))
    sq = []
    for qd in range(cn // quad):
        blocks = list(range(qd * bpq, (qd + 1) * bpq))
        bs = []
        b = SUBLANES
        while b < quad:
            bs.append(b)
            b *= 2
        sc = level(blocks, blocks, bs[-1])
        for b in reversed(bs[:-1]):
            sc = jnp.where(blkxor < 2 * b, level(blocks, blocks, b), sc)
        sq.append(sc.astype(BF16))

    vb = v.astype(BF16)
    if cn == quad:
        o = jnp.dot(sq[0], vb, preferred_element_type=F32)
    else:
        lo_blocks, hi_blocks = list(range(bpq)), list(range(bpq, nblk))
        cross = level(hi_blocks, lo_blocks, quad).astype(BF16)
        o = jnp.concatenate([
            jnp.dot(sq[0], vb[:quad], preferred_element_type=F32),
            jnp.dot(jnp.concatenate([cross, sq[1]], axis=1), vb, preferred_element_type=F32)], axis=0)

    r8 = lax.broadcasted_iota(jnp.int32, (SUBLANES, LANES), 0)
    od = [zero] * nblk
    for j in range(SUBLANES):
        for m in range(nblk):
            row = pl.ds(m * SUBLANES + j, 1)
            dec = jnp.exp2(jnp.where(r8 >= j, cb[m] - kc_s[row, :], -jnp.inf))
            od[m] = od[m] + jnp.sum(qb[m] * dec, axis=-1, keepdims=True) * v_s[row, :]

    o = (o + jnp.concatenate(od, axis=0)
         + lax.dot_general((q * jnp.exp2(c)).astype(BF16), st.astype(BF16), NT_DIMS, preferred_element_type=F32))

    clast = c[cn - 1:cn]
    kl = jnp.exp2(clast - kc).astype(BF16)
    st_next = jnp.exp2(clast) * st + lax.dot_general(vb, kl, TN_DIMS, preferred_element_type=F32)
    return _rmsnorm(o, nw) * (og * _sigmoid(og)), st_next


def _slab_blocks(r, c, nblocks):
    for ncb in range(1, nblocks + 1):
        if nblocks % ncb or c % ncb or r % (nblocks // ncb):
            continue
        br, bc = r // (nblocks // ncb), c // ncb
        if br % BF16_ROWS == 0 and bc % LANES == 0:
            return br, bc, ncb
    return None


def _cast_specs(weights, nt, nsteps):
    ins, outs = [], []
    for wt in weights:
        _, r, c = wt.shape
        k, (br, bc, ncb) = next((kk, _slab_blocks(r, c, nt * kk)) for kk in range(nsteps, 0, -1)
                                if _slab_blocks(r, c, nt * kk))
        blk = functools.partial(lambda i, j, k, ncb: divmod(i * k + jnp.minimum(j, k - 1), ncb), k=k, ncb=ncb)
        ins.append(pl.BlockSpec((None, br, bc), functools.partial(lambda i, j, blk: (0, *blk(i, j)), blk=blk)))
        outs.append(pl.BlockSpec((br, bc), blk))
    return ins, outs


def _mixer_kernel(sink_ref, pos_ref, invf_ref, m2_ref, x_ref, nw_ref, w_ref, lbp_ref, hnw_ref, *rest,
                  plan, n_hin, ncast, nb_seq, nt_seq, u_early, u_gate):
    cast_in, rest = rest[:ncast], rest[ncast:]
    g_ref, oa_ref, ob_ref = rest[:3]
    cast_out, rest = rest[3:3 + ncast], rest[3 + ncast:]
    h_ref, tab_ref, qkv_s, hin_s, k_s, v_s, st_ref, c_s, kc_s, v2_s = rest
    i, j = pl.program_id(0), pl.program_id(1)
    tm = x_ref.shape[0]
    n_rope = len(plan)
    cn = c_s.shape[1]

    for w_in_ref, w_out_ref in zip(cast_in, cast_out):
        w_out_ref[...] = w_in_ref[...].astype(w_out_ref.dtype)

    @pl.when((i == 0) & (j == 0))
    def _():
        k_s[...] = jnp.zeros_like(k_s)
        v_s[...] = jnp.zeros_like(v_s)
        st_ref[...] = jnp.zeros_like(st_ref)

    @pl.when(j == 0)
    def _():
        h_ref[...] = _rmsnorm(x_ref[...], nw_ref[...]).astype(BF16)
        _rope_tables(pos_ref, invf_ref, m2_ref, tab_ref)

    def project():
        return jnp.dot(h_ref[...], w_ref[...], preferred_element_type=F32)

    for jj, kinds in enumerate(plan):
        @pl.when(j == jj)
        def _(jj=jj, kinds=kinds):
            qkv_s[jj] = _rope_tile(project(), tab_ref, kinds).astype(BF16)

    hpt = PROJ_TILE // B_DIM
    nchunk = tm // cn
    fresh = i % nt_seq == 0

    def stash_and_attend(u):
        r = project().astype(BF16)
        for hh in range(hpt):
            hin_s[(u % 4) * B_HEADS + (u // 4) * hpt + hh] = r[:, hh * B_DIM:(hh + 1) * B_DIM]
        rows = pl.ds(pl.multiple_of(u * WINDOW, WINDOW), WINDOW)
        first = (i * n_hin + u) % nb_seq == 0
        q_at = lambda col: qkv_s[col // PROJ_TILE, rows, col % PROJ_TILE:col % PROJ_TILE + LANES]

        def store(col, val):
            oa_ref[rows, col:col + LANES] = val.astype(oa_ref.dtype)

        _swa_block(sink_ref, q_at, qkv_s[n_rope - 1, rows, :], first, k_s, v_s, store)

    def hgrn_units(base, count):
        ids = [base + e for e in range(count)]
        heads = [idx // (hpt * nchunk) * hpt + idx % hpt for idx in ids]
        chunks = [idx % (hpt * nchunk) // hpt for idx in ids]
        sts = [jnp.where(fresh & (ck == 0), 0.0, st_ref[hd]) for hd, ck in zip(heads, chunks)]
        for e, (hd, ck) in enumerate(zip(heads, chunks)):
            rows = pl.ds(pl.multiple_of(ck * cn, cn), cn)
            part = lambda pi: hin_s[pi * B_HEADS + hd, rows, :].astype(F32)
            o, sts[e] = _hgrn_chunk(lbp_ref[hd], hnw_ref[hd], part(0), part(1), part(2), part(3), sts[e],
                                    c_s.at[e], kc_s.at[e], v2_s.at[e])
            ob_ref[hd, rows, :] = o.astype(ob_ref.dtype)
        for hd, st in zip(heads, sts):
            st_ref[hd] = st

    @pl.when((j >= n_rope) & (j < n_rope + 4))
    def _():
        stash_and_attend(j - n_rope)

    @pl.when((j >= n_rope + 4) & (j < n_rope + n_hin))
    def _():
        stash_and_attend(j - n_rope)
        hgrn_units((j - n_rope - 4) * u_early, u_early)

    @pl.when(j >= n_rope + n_hin)
    def _():
        g_ref[...] = project().astype(g_ref.dtype)
        hgrn_units((n_hin - 4) * u_early + (j - n_rope - n_hin) * u_gate, u_gate)


def _mixers(x, nw, w, pos3, invf8, m2, sinks, lbp, hnw, weights, tm, seq):
    t, d = x.shape
    ntile, _, tn = w.shape
    assert tn == PROJ_TILE and A_WIDTH % tn == 0 and 2 * A_KV_WIDTH == tn and seq % tm == 0
    kind = lambda col: "q" if col < A_WIDTH else ("k" if col < A_WIDTH + A_KV_WIDTH else None)
    plan = [tuple(kind(jj * tn + gi * LANES) for gi in range(tn // LANES)) for jj in range(ntile)]
    plan = plan[:max(jj + 1 for jj, kinds in enumerate(plan) if any(kinds))]
    n_rope = len(plan)
    n_hin = 4 * B_WIDTH // tn
    n_gate = ntile - n_rope - n_hin
    cn = min(HGRN_CHUNK, tm)
    nt = t // tm
    u_early = 0
    units = B_HEADS * (tm // cn)
    u_gate = (units - (n_hin - 4) * u_early) // n_gate
    assert tm // WINDOW == n_hin and n_hin == 4 * B_WIDTH // tn == 8 and tn // B_DIM == 4
    assert (n_hin - 4) * u_early + n_gate * u_gate == units and (n_hin - 4) * u_early <= units // 2
    assert max(u_early, u_gate) <= tn // B_DIM
    cast_in, cast_out = _cast_specs(weights, nt, ntile)
    const2 = lambda i, j: (0, 0)
    const3 = lambda i, j: (0, 0, 0)
    outs = pl.pallas_call(
        functools.partial(_mixer_kernel, plan=plan, n_hin=n_hin, ncast=len(weights),
                          nb_seq=seq // WINDOW, nt_seq=seq // tm, u_early=u_early, u_gate=u_gate),
        grid=(nt, ntile),
        in_specs=[
            pl.BlockSpec(memory_space=pltpu.SMEM),
            pl.BlockSpec((tm // LANES, 1, LANES), lambda i, j: (i, 0, 0)),
            pl.BlockSpec(invf8.shape, const2),
            pl.BlockSpec(m2.shape, const2),
            pl.BlockSpec((tm, d), lambda i, j: (i, 0)),
            pl.BlockSpec((1, d), const2),
            pl.BlockSpec((None, d, tn), lambda i, j: (j, 0, 0)),
            pl.BlockSpec(lbp.shape, const3),
            pl.BlockSpec(hnw.shape, const3),
            *cast_in,
        ],
        out_specs=[
            pl.BlockSpec((None, tm, tn), lambda i, j: (jnp.maximum(j - n_rope - n_hin, 0), i, 0)),
            pl.BlockSpec((tm, A_WIDTH), lambda i, j: (i, 0)),
            pl.BlockSpec((B_HEADS, tm, B_DIM), lambda i, j: (0, i, 0)),
            *cast_out,
        ],
        out_shape=[
            jax.ShapeDtypeStruct((n_gate, t, tn), BF16),
            jax.ShapeDtypeStruct((t, A_WIDTH), BF16),
            jax.ShapeDtypeStruct((B_HEADS, t, B_DIM), BF16),
            *[jax.ShapeDtypeStruct(wt.shape[1:], BF16) for wt in weights],
        ],
        scratch_shapes=[
            pltpu.VMEM((tm, d), BF16),
            pltpu.VMEM((tm, 3 * LANES), F32),
            pltpu.VMEM((n_rope, tm, tn), BF16),
            pltpu.VMEM((4 * B_HEADS, tm, B_DIM), BF16),
            pltpu.VMEM((A_KV_HEADS, 2 * WINDOW, LANES), BF16),
            pltpu.VMEM((A_KV_HEADS, 2 * WINDOW, LANES), BF16),
            pltpu.VMEM((B_HEADS, B_DIM, B_DIM), F32),
            pltpu.VMEM((max(u_early, u_gate), cn, B_DIM), F32),
            pltpu.VMEM((max(u_early, u_gate), cn, B_DIM), F32),
            pltpu.VMEM((max(u_early, u_gate), cn, B_DIM), F32),
        ],
        compiler_params=pltpu.CompilerParams(dimension_semantics=("arbitrary", "arbitrary"),
                                             vmem_limit_bytes=MIXER_VMEM_LIMIT),
        name="mixers",
    )(sinks, pos3, invf8, m2, x, nw, w, lbp, hnw, *weights)
    return outs[0], outs[1], outs[2], list(outs[3:])


def _merge_kernel(*refs, nchunk):
    x_ref, oa_ref, ob_ref = refs[:3]
    ga_refs, gb_refs = refs[3:3 + nchunk], refs[3 + nchunk:3 + 2 * nchunk]
    wa_ref, wb_ref, wo_ref, o_ref = refs[3 + 2 * nchunk:]
    tn = ga_refs[0].shape[1]
    o_ref[...] = x_ref[...]
    oa = oa_ref[...]
    ob = jnp.concatenate([ob_ref[hd] for hd in range(ob_ref.shape[0])], axis=1)
    for c in range(nchunk):
        sl = slice(c * tn, (c + 1) * tn)
        ma = jnp.dot(oa, wa_ref[:, sl], preferred_element_type=F32)
        mb = jnp.dot(ob, wb_ref[:, sl], preferred_element_type=F32)
        merged = _sigmoid(ga_refs[c][...].astype(F32)) * ma + _sigmoid(gb_refs[c][...].astype(F32)) * mb
        o_ref[...] += jnp.dot(merged.astype(BF16), wo_ref[sl, :], preferred_element_type=F32)


def _merge(x, out_a, out_b, gates, wa, wb, wo, tm):
    t, d = x.shape
    tn = PROJ_TILE
    nchunk = d // tn
    gate = lambda c0: [pl.BlockSpec((None, tm, tn), functools.partial(lambda i, cc: (cc, i, 0), cc=c0 + c))
                       for c in range(nchunk)]
    return pl.pallas_call(
        functools.partial(_merge_kernel, nchunk=nchunk),
        grid=(t // tm,),
        in_specs=[
            pl.BlockSpec((tm, d), lambda i: (i, 0)),
            pl.BlockSpec((tm, A_WIDTH), lambda i: (i, 0)),
            pl.BlockSpec((B_HEADS, tm, B_DIM), lambda i: (0, i, 0)),
            *gate(0), *gate(nchunk),
            _resident(wa.shape), _resident(wb.shape), _resident(wo.shape),
        ],
        out_specs=pl.BlockSpec((tm, d), lambda i: (i, 0)),
        out_shape=jax.ShapeDtypeStruct((t, d), F32),
        compiler_params=_params("parallel"),
        name="merge",
    )(x, out_a, out_b, *([gates] * (2 * nchunk)), wa, wb, wo)


PLE_SUBTILES = 2


def _ple_kernel(x_ref, p_ref, nw_ref, fw_ref, wg_ref, wp_ref, o_ref, *, tn):
    d = x_ref.shape[1]
    rs = x_ref.shape[0] // PLE_SUBTILES
    for s in range(PLE_SUBTILES):
        rows = slice(s * rs, (s + 1) * rs)
        x = x_ref[rows, :]
        h = _rmsnorm(x, nw_ref[...]).astype(BF16)
        pb = p_ref[rows, :].astype(BF16)
        ss = jnp.zeros((rs, 1), F32)
        for c in range(d // tn):
            sl = slice(c * tn, (c + 1) * tn)
            g = _sigmoid(jnp.dot(h, wg_ref[:, sl], preferred_element_type=F32))
            y = x[:, sl] + g * jnp.dot(pb, wp_ref[:, sl], preferred_element_type=F32)
            ss = ss + jnp.sum(y * y, axis=-1, keepdims=True)
            o_ref[rows, sl] = y
        o_ref[rows, :] = o_ref[rows, :] * lax.rsqrt(ss * (1.0 / d) + EPS) * fw_ref[...]


def _ple(x, p, nw, fw, wg, wp, tm, tn):
    t, d = x.shape
    pd = p.shape[1]
    return pl.pallas_call(
        functools.partial(_ple_kernel, tn=tn),
        grid=(t // tm,),
        in_specs=[
            pl.BlockSpec((tm, d), lambda i: (i, 0)),
            pl.BlockSpec((tm, pd), lambda i: (i, 0)),
            pl.BlockSpec((1, d), lambda i: (0, 0)),
            pl.BlockSpec((1, d), lambda i: (0, 0)),
            _resident(wg.shape), _resident(wp.shape),
        ],
        out_specs=pl.BlockSpec((tm, d), lambda i: (i, 0)),
        out_shape=jax.ShapeDtypeStruct((t, d), F32),
        compiler_params=_params("parallel"),
        name="ple",
    )(x, p, nw, fw, wg, wp)


def _tiles(t, d, ff):
    return dict(ffn_tm=min(1024, t), ffn_tf=min(512, ff), ffn_first_tf=min(256, ff), mix_tm=min(1024, t),
                merge_tm=min(512, t), ple_tm=min(512, t), ple_tn=min(512, d))


def kernel(x, p, positions, ffn1_norm, ffn1_w_gate, ffn1_w_up, ffn1_w_down, mix_norm, w_in, attn_sinks, hgrn_lower_bound, hgrn_norm, w_up_a, w_up_b, w_out, ffn2_norm, ffn2_w_gate, ffn2_w_up, ffn2_w_down, ple_norm, ple_w_gate, ple_w_proj, final_norm):
    batch, seq, d = x.shape
    assert ffn1_norm.shape[0] == 1, "single-layer stack"
    t = batch * seq
    ts = _tiles(t, d, ffn1_w_gate.shape[-1])
    bf = lambda wt: wt.astype(BF16)
    row = lambda wt: wt.reshape(1, -1).astype(F32)

    inv_freq = jnp.power(jnp.float32(ROPE_THETA), -jnp.arange(0, ROT_DIM, 2, dtype=F32) / ROT_DIM)
    invf8 = jnp.broadcast_to(inv_freq[:, None], (ROT_HALF, LANES)).astype(F32)
    m2 = jnp.asarray(_rope_matrix(), BF16)
    pos3 = positions.reshape(t // LANES, 1, LANES)
    lbp = hgrn_lower_bound.astype(F32).reshape(-1, B_HEADS, B_DIM).transpose(1, 0, 2)
    hnw = hgrn_norm[0].astype(F32).reshape(B_HEADS, 1, B_DIM)

    x0 = x.reshape(t, d)
    n_rope = (A_WIDTH + 2 * A_KV_WIDTH) // PROJ_TILE
    halves = B_WIDTH // PROJ_TILE
    tile_order = (list(range(n_rope)) + [n_rope + part * halves + half for half in range(halves) for part in range(4)]
                  + list(range(n_rope + 4 * halves, w_in.shape[-1] // PROJ_TILE)))
    x1_first, wg1, wu1, wd1 = _ffn_first(x0, row(ffn1_norm[0]), ffn1_w_gate, ffn1_w_up, ffn1_w_down,
                                         ts["ffn_tm"], ts["ffn_first_tf"])
    x1, w_in_t = _ffn(x0, row(ffn1_norm[0]), wg1, wu1, wd1, ts["ffn_tm"], ts["ffn_tf"],
                      w_next=w_in, tn=PROJ_TILE, tile_order=tile_order, first=x1_first)
    later = [ffn2_w_gate, ffn2_w_up, ffn2_w_down]
    gates, out_a, out_b, (wg2, wu2, wd2) = _mixers(
        x1, row(mix_norm[0]), w_in_t, pos3, invf8, m2, attn_sinks[0].astype(F32), lbp, hnw, later,
        ts["mix_tm"], seq)
    x2 = _merge(x1, out_a, out_b, gates, bf(w_up_a[0]), bf(w_up_b[0]), bf(w_out[0]), ts["merge_tm"])
    x3 = _ffn(x2, row(ffn2_norm[0]), wg2, wu2, wd2, ts["ffn_tm"], ts["ffn_tf"])
    out = _ple(x3, p[0].reshape(t, -1), row(ple_norm[0]), row(final_norm), bf(ple_w_gate[0]), bf(ple_w_proj[0]),
               ts["ple_tm"], ts["ple_tn"])
    return out.reshape(batch, seq, d)
```

```python
import functools

import jax
import jax.numpy as jnp
import numpy as np
from jax import lax
from jax.experimental import pallas as pl
from jax.experimental.pallas import tpu as pltpu

EPS = 1e-6
A_HEADS = 16
A_KV_HEADS = 4
A_HEAD_DIM = 64
A_WIDTH = A_HEADS * A_HEAD_DIM
A_KV_WIDTH = A_KV_HEADS * A_HEAD_DIM
WINDOW = 128
ROT_DIM = A_HEAD_DIM // 4
ROT_HALF = ROT_DIM // 2
ROPE_THETA = 500000.0
B_HEADS = 8
B_DIM = 128
B_WIDTH = B_HEADS * B_DIM

LANES = 128
SUBLANES = 8
BF16_ROWS = 2 * SUBLANES
PROJ_TILE = 512
VMEM_LIMIT = 56 * 1024 * 1024
MIXER_VMEM_LIMIT = 58 * 1024 * 1024
LOG2E = 1.4426950408889634
Q_SCALE = A_HEAD_DIM ** -0.5 * LOG2E

F32 = jnp.float32
BF16 = jnp.bfloat16
NT_DIMS = (((1,), (1,)), ((), ()))
TN_DIMS = (((0,), (0,)), ((), ()))


def _params(*sem, vmem=VMEM_LIMIT):
    return pltpu.CompilerParams(dimension_semantics=sem, vmem_limit_bytes=vmem)


def _resident(shape):
    return pl.BlockSpec(shape, lambda *_: (0,) * len(shape), pipeline_mode=pl.Buffered(1))


def _rmsnorm(x, w):
    return x * lax.rsqrt(jnp.mean(x * x, axis=-1, keepdims=True) + EPS) * w


def _sigmoid(x):
    return 1.0 / (1.0 + jnp.exp(-x))


def _ffn_step(x_ref, nw_ref, wg, wu, wd, o_ref, h_ref, active=None):
    first = pl.program_id(1) == 0
    on = (lambda c: c) if active is None else (lambda c: c & active)

    @pl.when(on(first))
    def _():
        x = x_ref[...]
        h_ref[...] = _rmsnorm(x, nw_ref[...]).astype(BF16)
        o_ref[...] = x

    def body():
        h = h_ref[...]
        g = jnp.dot(h, wg[...], preferred_element_type=F32)
        u = jnp.dot(h, wu[...], preferred_element_type=F32)
        a = (0.5 * g * _sigmoid(g) * u).astype(BF16)
        o_ref[...] += jnp.dot(a, wd[...], preferred_element_type=F32)

    if active is None:
        body()
    else:
        pl.when(active)(body)


def _ffn_first_kernel(x_ref, nw_ref, wg_ref, wu_ref, wd_ref, o_ref, wgo_ref, wuo_ref, wdo_ref, h_ref):
    wgo_ref[...] = wg_ref[...].astype(BF16)
    wuo_ref[...] = wu_ref[...].astype(BF16)
    wdo_ref[...] = wd_ref[...].astype(BF16)
    _ffn_step(x_ref, nw_ref, wgo_ref, wuo_ref, wdo_ref, o_ref, h_ref)


def _ffn_first(x, nw, wg, wu, wd, tm, tf):
    t, d = x.shape
    ff = wg.shape[2]
    return pl.pallas_call(
        _ffn_first_kernel,
        grid=(1, ff // tf),
        in_specs=[
            pl.BlockSpec((tm, d), lambda i, f: (0, 0), pipeline_mode=pl.Buffered(1)),
            pl.BlockSpec((1, d), lambda i, f: (0, 0)),
            pl.BlockSpec((None, d, tf), lambda i, f: (0, 0, f)),
            pl.BlockSpec((None, d, tf), lambda i, f: (0, 0, f)),
            pl.BlockSpec((None, tf, d), lambda i, f: (0, f, 0)),
        ],
        out_specs=[
            pl.BlockSpec((tm, d), lambda i, f: (0, 0)),
            pl.BlockSpec((d, tf), lambda i, f: (0, f)),
            pl.BlockSpec((d, tf), lambda i, f: (0, f)),
            pl.BlockSpec((tf, d), lambda i, f: (f, 0)),
        ],
        out_shape=[jax.ShapeDtypeStruct((tm, d), F32), jax.ShapeDtypeStruct((d, ff), BF16),
                   jax.ShapeDtypeStruct((d, ff), BF16), jax.ShapeDtypeStruct((ff, d), BF16)],
        scratch_shapes=[pltpu.VMEM((tm, d), BF16)],
        compiler_params=_params("arbitrary", "arbitrary"),
        name="ffn_first",
    )(x, nw, wg, wu, wd)


def _ffn_kernel(x_ref, nw_ref, wg_ref, wu_ref, wd_ref, *rest, tile_order=None, has_first=False):
    rest = list(rest)
    first_ref = rest.pop(0) if has_first else None
    wn_ref = rest.pop(0) if tile_order is not None else None
    o_ref = rest.pop(0)
    wno_ref = rest.pop(0) if tile_order is not None else None
    h_ref = rest.pop(0)
    if wn_ref is not None:
        tn = wno_ref.shape[2]
        for k, jt in enumerate(tile_order):
            wno_ref[k] = wn_ref[:, jt * tn:(jt + 1) * tn].astype(wno_ref.dtype)

    if not has_first:
        _ffn_step(x_ref, nw_ref, wg_ref, wu_ref, wd_ref, o_ref, h_ref)
        return

    (sem,) = rest
    i, f = pl.program_id(0), pl.program_id(1)

    @pl.when((i == 0) & (f == 0))
    def _():
        cp = pltpu.make_async_copy(first_ref, o_ref, sem)
        cp.start()
        cp.wait()

    _ffn_step(x_ref, nw_ref, wg_ref, wu_ref, wd_ref, o_ref, h_ref, active=i > 0)


def _ffn(x, nw, wg, wu, wd, tm, tf, w_next=None, tn=None, tile_order=None, first=None):
    t, d = x.shape
    ff = wg.shape[1]
    nt, nf = t // tm, ff // tf
    ftile = (lambda i, f: f) if first is None else (lambda i, f: jnp.where(i == 0, 0, f))
    xtile = (lambda i: i) if first is None else (lambda i: jnp.maximum(i, 1))
    in_specs = [
        pl.BlockSpec((tm, d), lambda i, f: (xtile(i), 0)),
        pl.BlockSpec((1, d), lambda i, f: (0, 0)),
        pl.BlockSpec((d, tf), lambda i, f: (0, ftile(i, f))),
        pl.BlockSpec((d, tf), lambda i, f: (0, ftile(i, f))),
        pl.BlockSpec((tf, d), lambda i, f: (ftile(i, f), 0)),
    ]
    out_specs = pl.BlockSpec((tm, d), lambda i, f: (i, 0))
    out_shape = jax.ShapeDtypeStruct((t, d), F32)
    args = (x, nw, wg, wu, wd)
    scratch = [pltpu.VMEM((tm, d), BF16)]
    if first is not None:
        in_specs.append(pl.BlockSpec(memory_space=pl.ANY))
        args += (first,)
        scratch.append(pltpu.SemaphoreType.DMA(()))
    if w_next is not None:
        _, r, c = w_next.shape
        k = max(kk for kk in range(1, nf + 1) if r % (nt * kk) == 0 and (r // (nt * kk)) % BF16_ROWS == 0)
        rps = r // (nt * k)
        slab = lambda i, f: (0, i * k + jnp.minimum(f, k - 1), 0)
        in_specs.append(pl.BlockSpec((None, rps, c), slab))
        out_specs = [out_specs, pl.BlockSpec((c // tn, rps, tn), slab)]
        out_shape = [out_shape, jax.ShapeDtypeStruct((c // tn, r, tn), BF16)]
        args += (w_next,)
    return pl.pallas_call(
        functools.partial(_ffn_kernel, tile_order=tile_order, has_first=first is not None),
        grid=(nt, nf),
        in_specs=in_specs,
        out_specs=out_specs,
        out_shape=out_shape,
        scratch_shapes=scratch,
        compiler_params=_params("arbitrary", "arbitrary", vmem=MIXER_VMEM_LIMIT),
        name="ffn",
    )(*args)


def _rope_matrix():
    m = np.zeros((2 * ROT_HALF, 3 * LANES), np.float32)
    for lane in range(LANES):
        dd = lane % A_HEAD_DIM
        if dd < ROT_DIM:
            m[dd % ROT_HALF, lane] = 1.0
        if dd < ROT_HALF:
            m[ROT_HALF + dd, LANES + lane] = -1.0
        elif dd < ROT_DIM:
            m[ROT_HALF + dd - ROT_HALF, 2 * LANES + lane] = 1.0
    return np.concatenate([m, m], axis=0)


def _rope_tables(pos_ref, invf_ref, m2_ref, tab_ref):
    for gi in range(tab_ref.shape[0] // LANES):
        ang = invf_ref[...] * pos_ref[gi].astype(F32)
        cs = jnp.concatenate([jnp.cos(ang) - 1.0, jnp.sin(ang)], axis=0)
        hi = cs.astype(BF16)
        lo = (cs - hi.astype(F32)).astype(BF16)
        tab_ref[gi * LANES:(gi + 1) * LANES, :] = lax.dot_general(
            jnp.concatenate([hi, lo], axis=0), m2_ref[...], TN_DIMS, preferred_element_type=F32)


def _rope_tile(r, tab_ref, kinds):
    cs = tab_ref[:, 0:LANES] + 1.0
    s1 = tab_ref[:, LANES:2 * LANES]
    s2 = tab_ref[:, 2 * LANES:3 * LANES]
    out = []
    for gi, kind in enumerate(kinds):
        rg = r[:, gi * LANES:(gi + 1) * LANES]
        if kind is not None:
            rg = rg * cs + pltpu.roll(rg, LANES - ROT_HALF, 1) * s1 + pltpu.roll(rg, ROT_HALF, 1) * s2
        if kind == "q":
            rg = rg * Q_SCALE
        out.append(rg)
    return jnp.concatenate(out, axis=1)


def _swa_block(sink_ref, q_at, kv, first, k_s, v_s, store):
    blk = kv.shape[0]
    low = lax.broadcasted_iota(jnp.int32, (1, LANES), 1) < A_HEAD_DIM
    qi = lax.broadcasted_iota(jnp.int32, (blk, 2 * blk), 0)
    kj = lax.broadcasted_iota(jnp.int32, (blk, 2 * blk), 1)
    allowed = (kj > qi) & (kj <= qi + blk) & (jnp.logical_not(first) | (kj >= blk))

    grp = A_HEADS // A_KV_HEADS
    for g in range(A_KV_HEADS):
        c0 = (g // 2) * LANES
        own = low if g % 2 == 0 else jnp.logical_not(low)
        kc = kv[:, c0:c0 + LANES].astype(F32)
        vc = kv[:, A_KV_WIDTH + c0:A_KV_WIDTH + c0 + LANES].astype(F32)
        k2 = jnp.where(own, kc, pltpu.roll(kc, A_HEAD_DIM, 1)).astype(BF16)
        v2 = jnp.where(own, vc, pltpu.roll(vc, A_HEAD_DIM, 1)).astype(BF16)
        k_s[g, blk:, :] = k2
        v_s[g, blk:, :] = v2
        qs = []
        for pr in range(grp // 2):
            qp = q_at((g * grp + 2 * pr) * A_HEAD_DIM).astype(F32)
            qs += [jnp.where(low, qp, 0.0), jnp.where(low, 0.0, qp)]
        qst = jnp.concatenate(qs, axis=0).astype(BF16)
        s = lax.dot_general(qst, k_s[g], NT_DIMS, preferred_element_type=F32)
        es, rden = [], []
        for r in range(grp):
            sk = sink_ref[g * grp + r] * LOG2E
            sr = jnp.where(allowed, s[r * blk:(r + 1) * blk], -jnp.inf)
            m = jnp.maximum(jnp.max(sr, axis=-1, keepdims=True), sk)
            e = jnp.exp2(sr - m)
            rden.append(1.0 / (jnp.sum(e, axis=-1, keepdims=True) + jnp.exp2(sk - m)))
            es.append(e.astype(BF16))
        o = jnp.dot(jnp.concatenate(es, axis=0), v_s[g], preferred_element_type=F32)
        for pr in range(grp // 2):
            oa = o[(2 * pr) * blk:(2 * pr + 1) * blk] * rden[2 * pr]
            ob = o[(2 * pr + 1) * blk:(2 * pr + 2) * blk] * rden[2 * pr + 1]
            store((g * grp + 2 * pr) * A_HEAD_DIM, jnp.where(low, oa, ob))
        k_s[g, :blk, :] = k2
        v_s[g, :blk, :] = v2


HGRN_CHUNK = 256
HGRN_QUAD = 128


def _hgrn_chunk(a, nw, qp, fp, v, og, st, c_s, kc_s, v_s):
    cn = qp.shape[0]
    nblk = cn // SUBLANES
    quad = min(HGRN_QUAD, cn)
    bpq = quad // SUBLANES
    assert cn in (quad, 2 * quad)

    ea = jnp.exp(a - jnp.max(a, axis=0, keepdims=True))
    lb = ea[0:1] / jnp.sum(ea, axis=0, keepdims=True)

    q = qp * _sigmoid(qp)
    fg = lb + (1.0 - lb) * _sigmoid(fp)
    v_s[...] = v

    row8 = lax.broadcasted_iota(jnp.int32, (cn, LANES), 0) % SUBLANES
    w = jnp.log(fg) * LOG2E
    for sh in (1, 2, 4):
        w = w + jnp.where(row8 >= sh, pltpu.roll(w, sh, 0), 0.0)
    c_s[...] = w
    run = jnp.zeros((SUBLANES, LANES), F32)
    cb = []
    for m in range(nblk):
        cb.append(w[m * SUBLANES:(m + 1) * SUBLANES] + run)
        run = run + c_s[pl.ds(m * SUBLANES + SUBLANES - 1, 1), :]
    c = jnp.concatenate(cb, axis=0)
    c_s[...] = c
    kc = c - jnp.log(1.0 - fg) * LOG2E
    kc_s[...] = kc
    qb = [q[m * SUBLANES:(m + 1) * SUBLANES] for m in range(nblk)]
    kcb = [kc[m * SUBLANES:(m + 1) * SUBLANES] for m in range(nblk)]
    zero = jnp.zeros((SUBLANES, LANES), F32)

    def level(blocks_q, blocks_k, b):
        qt, kt = [], []
        for m in blocks_q:
            anchor = (m * SUBLANES) // (2 * b) * (2 * b) + b
            if m * SUBLANES >= anchor:
                qt.append(qb[m] * jnp.exp2(cb[m] - c_s[pl.ds(anchor - 1, 1), :]))
            else:
                qt.append(zero)
        for m in blocks_k:
            anchor = (m * SUBLANES) // (2 * b) * (2 * b) + b
            if m * SUBLANES < anchor:
                kt.append(jnp.exp2(c_s[pl.ds(anchor - 1, 1), :] - kcb[m]))
            else:
                kt.append(zero)
        return lax.dot_general(jnp.concatenate(qt, axis=0).astype(BF16), jnp.concatenate(kt, axis=0).astype(BF16),
                               NT_DIMS, preferred_element_type=F32)

    blkxor = (lax.broadcasted_iota(jnp.int32, (quad, quad), 0) ^ lax.broadcasted_iota(jnp.int32, (quad, quad), 1))
    sq = []
    for qd in range(cn // quad):
        blocks = list(range(qd * bpq, (qd + 1) * bpq))
        bs = []
        b = SUBLANES
        while b < quad:
            bs.append(b)
            b *= 2
        sc = level(blocks, blocks, bs[-1])
        for b in reversed(bs[:-1]):
            sc = jnp.where(blkxor < 2 * b, level(blocks, blocks, b), sc)
        sq.append(sc.astype(BF16))

    vb = v.astype(BF16)
    if cn == quad:
        o = jnp.dot(sq[0], vb, preferred_element_type=F32)
    else:
        lo_blocks, hi_blocks = list(range(bpq)), list(range(bpq, nblk))
        cross = level(hi_blocks, lo_blocks, quad).astype(BF16)
        o = jnp.concatenate([
            jnp.dot(sq[0], vb[:quad], preferred_element_type=F32),
            jnp.dot(jnp.concatenate([cross, sq[1]], axis=1), vb, preferred_element_type=F32)], axis=0)

    r8 = lax.broadcasted_iota(jnp.int32, (SUBLANES, LANES), 0)
    od = [zero] * nblk
    for j in range(SUBLANES):
        for m in range(nblk):
            row = pl.ds(m * SUBLANES + j, 1)
            dec = jnp.exp2(jnp.where(r8 >= j, cb[m] - kc_s[row, :], -jnp.inf))
            od[m] = od[m] + jnp.sum(qb[m] * dec, axis=-1, keepdims=True) * v_s[row, :]

    o = (o + jnp.concatenate(od, axis=0)
         + lax.dot_general((q * jnp.exp2(c)).astype(BF16), st.astype(BF16), NT_DIMS, preferred_element_type=F32))

    clast = c[cn - 1:cn]
    kl = jnp.exp2(clast - kc).astype(BF16)
    st_next = jnp.exp2(clast) * st + lax.dot_general(vb, kl, TN_DIMS, preferred_element_type=F32)
    return _rmsnorm(o, nw) * (og * _sigmoid(og)), st_next


def _slab_blocks(r, c, nblocks):
    for ncb in range(1, nblocks + 1):
        if nblocks % ncb or c % ncb or r % (nblocks // ncb):
            continue
        br, bc = r // (nblocks // ncb), c // ncb
        if br % BF16_ROWS == 0 and bc % LANES == 0:
            return br, bc, ncb
    return None


def _cast_specs(weights, nt, nsteps):
    ins, outs = [], []
    for wt in weights:
        _, r, c = wt.shape
        k, (br, bc, ncb) = next((kk, _slab_blocks(r, c, nt * kk)) for kk in range(nsteps, 0, -1)
                                if _slab_blocks(r, c, nt * kk))
        blk = functools.partial(lambda i, j, k, ncb: divmod(i * k + jnp.minimum(j, k - 1), ncb), k=k, ncb=ncb)
        ins.append(pl.BlockSpec((None, br, bc), functools.partial(lambda i, j, blk: (0, *blk(i, j)), blk=blk)))
        outs.append(pl.BlockSpec((br, bc), blk))
    return ins, outs


def _mixer_kernel(sink_ref, pos_ref, invf_ref, m2_ref, x_ref, nw_ref, w_ref, lbp_ref, hnw_ref, *rest,
                  plan, n_hin, ncast, nb_seq, nt_seq, u_early, u_gate):
    cast_in, rest = rest[:ncast], rest[ncast:]
    g_ref, oa_ref, ob_ref = rest[:3]
    cast_out, rest = rest[3:3 + ncast], rest[3 + ncast:]
    h_ref, tab_ref, qkv_s, hin_s, k_s, v_s, st_ref, c_s, kc_s, v2_s = rest
    i, j = pl.program_id(0), pl.program_id(1)
    tm = x_ref.shape[0]
    n_rope = len(plan)
    cn = c_s.shape[1]

    for w_in_ref, w_out_ref in zip(cast_in, cast_out):
        w_out_ref[...] = w_in_ref[...].astype(w_out_ref.dtype)

    @pl.when((i == 0) & (j == 0))
    def _():
        k_s[...] = jnp.zeros_like(k_s)
        v_s[...] = jnp.zeros_like(v_s)
        st_ref[...] = jnp.zeros_like(st_ref)

    @pl.when(j == 0)
    def _():
        h_ref[...] = _rmsnorm(x_ref[...], nw_ref[...]).astype(BF16)
        _rope_tables(pos_ref, invf_ref, m2_ref, tab_ref)

    def project():
        return jnp.dot(h_ref[...], w_ref[...], preferred_element_type=F32)

    for jj, kinds in enumerate(plan):
        @pl.when(j == jj)
        def _(jj=jj, kinds=kinds):
            qkv_s[jj] = _rope_tile(project(), tab_ref, kinds).astype(BF16)

    hpt = PROJ_TILE // B_DIM
    nchunk = tm // cn
    fresh = i % nt_seq == 0

    def stash_and_attend(u):
        r = project().astype(BF16)
        for hh in range(hpt):
            hin_s[(u % 4) * B_HEADS + (u // 4) * hpt + hh] = r[:, hh * B_DIM:(hh + 1) * B_DIM]
        rows = pl.ds(pl.multiple_of(u * WINDOW, WINDOW), WINDOW)
        first = (i * n_hin + u) % nb_seq == 0
        q_at = lambda col: qkv_s[col // PROJ_TILE, rows, col % PROJ_TILE:col % PROJ_TILE + LANES]

        def store(col, val):
            oa_ref[rows, col:col + LANES] = val.astype(oa_ref.dtype)

        _swa_block(sink_ref, q_at, qkv_s[n_rope - 1, rows, :], first, k_s, v_s, store)

    def hgrn_units(base, count):
        ids = [base + e for e in range(count)]
        heads = [idx // (hpt * nchunk) * hpt + idx % hpt for idx in ids]
        chunks = [idx % (hpt * nchunk) // hpt for idx in ids]
        sts = [jnp.where(fresh & (ck == 0), 0.0, st_ref[hd]) for hd, ck in zip(heads, chunks)]
        for e, (hd, ck) in enumerate(zip(heads, chunks)):
            rows = pl.ds(pl.multiple_of(ck * cn, cn), cn)
            part = lambda pi: hin_s[pi * B_HEADS + hd, rows, :].astype(F32)
            o, sts[e] = _hgrn_chunk(lbp_ref[hd], hnw_ref[hd], part(0), part(1), part(2), part(3), sts[e],
                                    c_s.at[e], kc_s.at[e], v2_s.at[e])
            ob_ref[hd, rows, :] = o.astype(ob_ref.dtype)
        for hd, st in zip(heads, sts):
            st_ref[hd] = st

    @pl.when((j >= n_rope) & (j < n_rope + 4))
    def _():
        stash_and_attend(j - n_rope)

    @pl.when((j >= n_rope + 4) & (j < n_rope + n_hin))
    def _():
        stash_and_attend(j - n_rope)
        hgrn_units((j - n_rope - 4) * u_early, u_early)

    @pl.when(j >= n_rope + n_hin)
    def _():
        g_ref[...] = project().astype(g_ref.dtype)
        hgrn_units((n_hin - 4) * u_early + (j - n_rope - n_hin) * u_gate, u_gate)


def _mixers(x, nw, w, pos3, invf8, m2, sinks, lbp, hnw, weights, tm, seq):
    t, d = x.shape
    ntile, _, tn = w.shape
    assert tn == PROJ_TILE and A_WIDTH % tn == 0 and 2 * A_KV_WIDTH == tn and seq % tm == 0
    kind = lambda col: "q" if col < A_WIDTH else ("k" if col < A_WIDTH + A_KV_WIDTH else None)
    plan = [tuple(kind(jj * tn + gi * LANES) for gi in range(tn // LANES)) for jj in range(ntile)]
    plan = plan[:max(jj + 1 for jj, kinds in enumerate(plan) if any(kinds))]
    n_rope = len(plan)
    n_hin = 4 * B_WIDTH // tn
    n_gate = ntile - n_rope - n_hin
    cn = min(HGRN_CHUNK, tm)
    nt = t // tm
    u_early = 0
    units = B_HEADS * (tm // cn)
    u_gate = (units - (n_hin - 4) * u_early) // n_gate
    assert tm // WINDOW == n_hin and n_hin == 4 * B_WIDTH // tn == 8 and tn // B_DIM == 4
    assert (n_hin - 4) * u_early + n_gate * u_gate == units and (n_hin - 4) * u_early <= units // 2
    assert max(u_early, u_gate) <= tn // B_DIM
    cast_in, cast_out = _cast_specs(weights, nt, ntile)
    const2 = lambda i, j: (0, 0)
    const3 = lambda i, j: (0, 0, 0)
    outs = pl.pallas_call(
        functools.partial(_mixer_kernel, plan=plan, n_hin=n_hin, ncast=len(weights),
                          nb_seq=seq // WINDOW, nt_seq=seq // tm, u_early=u_early, u_gate=u_gate),
        grid=(nt, ntile),
        in_specs=[
            pl.BlockSpec(memory_space=pltpu.SMEM),
            pl.BlockSpec((tm // LANES, 1, LANES), lambda i, j: (i, 0, 0)),
            pl.BlockSpec(invf8.shape, const2),
            pl.BlockSpec(m2.shape, const2),
            pl.BlockSpec((tm, d), lambda i, j: (i, 0)),
            pl.BlockSpec((1, d), const2),
            pl.BlockSpec((None, d, tn), lambda i, j: (j, 0, 0)),
            pl.BlockSpec(lbp.shape, const3),
            pl.BlockSpec(hnw.shape, const3),
            *cast_in,
        ],
        out_specs=[
            pl.BlockSpec((None, tm, tn), lambda i, j: (jnp.maximum(j - n_rope - n_hin, 0), i, 0)),
            pl.BlockSpec((tm, A_WIDTH), lambda i, j: (i, 0)),
            pl.BlockSpec((B_HEADS, tm, B_DIM), lambda i, j: (0, i, 0)),
            *cast_out,
        ],
        out_shape=[
            jax.ShapeDtypeStruct((n_gate, t, tn), BF16),
            jax.ShapeDtypeStruct((t, A_WIDTH), BF16),
            jax.ShapeDtypeStruct((B_HEADS, t, B_DIM), BF16),
            *[jax.ShapeDtypeStruct(wt.shape[1:], BF16) for wt in weights],
        ],
        scratch_shapes=[
            pltpu.VMEM((tm, d), BF16),
            pltpu.VMEM((tm, 3 * LANES), F32),
            pltpu.VMEM((n_rope, tm, tn), BF16),
            pltpu.VMEM((4 * B_HEADS, tm, B_DIM), BF16),
            pltpu.VMEM((A_KV_HEADS, 2 * WINDOW, LANES), BF16),
            pltpu.VMEM((A_KV_HEADS, 2 * WINDOW, LANES), BF16),
            pltpu.VMEM((B_HEADS, B_DIM, B_DIM), F32),
            pltpu.VMEM((max(u_early, u_gate), cn, B_DIM), F32),
            pltpu.VMEM((max(u_early, u_gate), cn, B_DIM), F32),
            pltpu.VMEM((max(u_early, u_gate), cn, B_DIM), F32),
        ],
        compiler_params=pltpu.CompilerParams(dimension_semantics=("arbitrary", "arbitrary"),
                                             vmem_limit_bytes=MIXER_VMEM_LIMIT),
        name="mixers",
    )(sinks, pos3, invf8, m2, x, nw, w, lbp, hnw, *weights)
    return outs[0], outs[1], outs[2], list(outs[3:])


def _merge_kernel(*refs, nchunk):
    x_ref, oa_ref, ob_ref = refs[:3]
    ga_refs, gb_refs = refs[3:3 + nchunk], refs[3 + nchunk:3 + 2 * nchunk]
    wa_ref, wb_ref, wo_ref, o_ref = refs[3 + 2 * nchunk:]
    tn = ga_refs[0].shape[1]
    o_ref[...] = x_ref[...]
    oa = oa_ref[...]
    ob = jnp.concatenate([ob_ref[hd] for hd in range(ob_ref.shape[0])], axis=1)
    for c in range(nchunk):
        sl = slice(c * tn, (c + 1) * tn)
        ma = jnp.dot(oa, wa_ref[:, sl], preferred_element_type=F32)
        mb = jnp.dot(ob, wb_ref[:, sl], preferred_element_type=F32)
        merged = _sigmoid(ga_refs[c][...].astype(F32)) * ma + _sigmoid(gb_refs[c][...].astype(F32)) * mb
        o_ref[...] += jnp.dot(merged.astype(BF16), wo_ref[sl, :], preferred_element_type=F32)


def _merge(x, out_a, out_b, gates, wa, wb, wo, tm):
    t, d = x.shape
    tn = PROJ_TILE
    nchunk = d // tn
    gate = lambda c0: [pl.BlockSpec((None, tm, tn), functools.partial(lambda i, cc: (cc, i, 0), cc=c0 + c))
                       for c in range(nchunk)]
    return pl.pallas_call(
        functools.partial(_merge_kernel, nchunk=nchunk),
        grid=(t // tm,),
        in_specs=[
            pl.BlockSpec((tm, d), lambda i: (i, 0)),
            pl.BlockSpec((tm, A_WIDTH), lambda i: (i, 0)),
            pl.BlockSpec((B_HEADS, tm, B_DIM), lambda i: (0, i, 0)),
            *gate(0), *gate(nchunk),
            _resident(wa.shape), _resident(wb.shape), _resident(wo.shape),
        ],
        out_specs=pl.BlockSpec((tm, d), lambda i: (i, 0)),
        out_shape=jax.ShapeDtypeStruct((t, d), F32),
        compiler_params=_params("parallel"),
        name="merge",
    )(x, out_a, out_b, *([gates] * (2 * nchunk)), wa, wb, wo)


PLE_SUBTILES = 2


def _ple_kernel(x_ref, p_ref, nw_ref, fw_ref, wg32_ref, wp32_ref, o_ref, wg_ref, wp_ref, *, tn):
    @pl.when(pl.program_id(0) == 0)
    def _():
        wg_ref[...] = wg32_ref[0].astype(BF16)
        wp_ref[...] = wp32_ref[0].astype(BF16)

    d = x_ref.shape[1]
    rs = x_ref.shape[0] // PLE_SUBTILES
    for s in range(PLE_SUBTILES):
        rows = slice(s * rs, (s + 1) * rs)
        x = x_ref[rows, :]
        h = _rmsnorm(x, nw_ref[...]).astype(BF16)
        pb = p_ref[rows, :].astype(BF16)
        ss = jnp.zeros((rs, 1), F32)
        for c in range(d // tn):
            sl = slice(c * tn, (c + 1) * tn)
            g = _sigmoid(jnp.dot(h, wg_ref[:, sl], preferred_element_type=F32))
            y = x[:, sl] + g * jnp.dot(pb, wp_ref[:, sl], preferred_element_type=F32)
            ss = ss + jnp.sum(y * y, axis=-1, keepdims=True)
            o_ref[rows, sl] = y
        o_ref[rows, :] = o_ref[rows, :] * lax.rsqrt(ss * (1.0 / d) + EPS) * fw_ref[...]


def _ple(x, p, nw, fw, wg, wp, tm, tn):
    t, d = x.shape
    pd = p.shape[1]
    return pl.pallas_call(
        functools.partial(_ple_kernel, tn=tn),
        grid=(t // tm,),
        in_specs=[
            pl.BlockSpec((tm, d), lambda i: (i, 0)),
            pl.BlockSpec((tm, pd), lambda i: (i, 0)),
            pl.BlockSpec((1, d), lambda i: (0, 0)),
            pl.BlockSpec((1, d), lambda i: (0, 0)),
            _resident(wg.shape), _resident(wp.shape),
        ],
        out_specs=pl.BlockSpec((tm, d), lambda i: (i, 0)),
        out_shape=jax.ShapeDtypeStruct((t, d), F32),
        scratch_shapes=[pltpu.VMEM((d, d), BF16), pltpu.VMEM((pd, d), BF16)],
        compiler_params=_params("arbitrary"),
        name="ple",
    )(x, p, nw, fw, wg, wp)


def _tiles(t, d, ff):
    return dict(ffn_tm=min(1024, t), ffn_tf=min(512, ff), ffn_first_tf=min(256, ff), mix_tm=min(1024, t),
                merge_tm=min(512, t), ple_tm=min(512, t), ple_tn=min(512, d))


def kernel(x, p, positions, ffn1_norm, ffn1_w_gate, ffn1_w_up, ffn1_w_down, mix_norm, w_in, attn_sinks, hgrn_lower_bound, hgrn_norm, w_up_a, w_up_b, w_out, ffn2_norm, ffn2_w_gate, ffn2_w_up, ffn2_w_down, ple_norm, ple_w_gate, ple_w_proj, final_norm):
    batch, seq, d = x.shape
    assert ffn1_norm.shape[0] == 1, "single-layer stack"
    t = batch * seq
    ts = _tiles(t, d, ffn1_w_gate.shape[-1])
    bf = lambda wt: wt.astype(BF16)
    row = lambda wt: wt.reshape(1, -1).astype(F32)

    inv_freq = jnp.power(jnp.float32(ROPE_THETA), -jnp.arange(0, ROT_DIM, 2, dtype=F32) / ROT_DIM)
    invf8 = jnp.broadcast_to(inv_freq[:, None], (ROT_HALF, LANES)).astype(F32)
    m2 = jnp.asarray(_rope_matrix(), BF16)
    pos3 = positions.reshape(t // LANES, 1, LANES)
    lbp = hgrn_lower_bound.astype(F32).reshape(-1, B_HEADS, B_DIM).transpose(1, 0, 2)
    hnw = hgrn_norm[0].astype(F32).reshape(B_HEADS, 1, B_DIM)

    x0 = x.reshape(t, d)
    n_rope = (A_WIDTH + 2 * A_KV_WIDTH) // PROJ_TILE
    halves = B_WIDTH // PROJ_TILE
    tile_order = (list(range(n_rope)) + [n_rope + part * halves + half for half in range(halves) for part in range(4)]
                  + list(range(n_rope + 4 * halves, w_in.shape[-1] // PROJ_TILE)))
    x1_first, wg1, wu1, wd1 = _ffn_first(x0, row(ffn1_norm[0]), ffn1_w_gate, ffn1_w_up, ffn1_w_down,
                                         ts["ffn_tm"], ts["ffn_first_tf"])
    x1, w_in_t = _ffn(x0, row(ffn1_norm[0]), wg1, wu1, wd1, ts["ffn_tm"], ts["ffn_tf"],
                      w_next=w_in, tn=PROJ_TILE, tile_order=tile_order, first=x1_first)
    later = [ffn2_w_gate, ffn2_w_up, ffn2_w_down]
    gates, out_a, out_b, (wg2, wu2, wd2) = _mixers(
        x1, row(mix_norm[0]), w_in_t, pos3, invf8, m2, attn_sinks[0].astype(F32), lbp, hnw, later,
        ts["mix_tm"], seq)
    x2 = _merge(x1, out_a, out_b, gates, bf(w_up_a[0]), bf(w_up_b[0]), bf(w_out[0]), ts["merge_tm"])
    x3 = _ffn(x2, row(ffn2_norm[0]), wg2, wu2, wd2, ts["ffn_tm"], ts["ffn_tf"])
    out = _ple(x3, p[0].reshape(t, -1), row(ple_norm[0]), row(final_norm), ple_w_gate, ple_w_proj,
               ts["ple_tm"], ts["ple_tn"])
    return out.reshape(batch, seq, d)
```

```python
import functools

import jax
import jax.numpy as jnp
import numpy as np
from jax import lax
from jax.experimental import pallas as pl
from jax.experimental.pallas import tpu as pltpu

EPS = 1e-6
A_HEADS = 16
A_KV_HEADS = 4
A_HEAD_DIM = 64
A_WIDTH = A_HEADS * A_HEAD_DIM
A_KV_WIDTH = A_KV_HEADS * A_HEAD_DIM
WINDOW = 128
ROT_DIM = A_HEAD_DIM // 4
ROT_HALF = ROT_DIM // 2
ROPE_THETA = 500000.0
B_HEADS = 8
B_DIM = 128
B_WIDTH = B_HEADS * B_DIM

LANES = 128
SUBLANES = 8
BF16_ROWS = 2 * SUBLANES
PROJ_TILE = 512
VMEM_LIMIT = 56 * 1024 * 1024
BIG_VMEM_LIMIT = 58 * 1024 * 1024
LOG2E = 1.4426950408889634
Q_SCALE = A_HEAD_DIM ** -0.5 * LOG2E

F32 = jnp.float32
BF16 = jnp.bfloat16
NT_DIMS = (((1,), (1,)), ((), ()))
TN_DIMS = (((0,), (0,)), ((), ()))


def _params(*sem, vmem=VMEM_LIMIT):
    return pltpu.CompilerParams(dimension_semantics=sem, vmem_limit_bytes=vmem)


def _resident(shape):
    return pl.BlockSpec(shape, lambda *_: (0,) * len(shape), pipeline_mode=pl.Buffered(1))


def _rmsnorm(x, w):
    return x * lax.rsqrt(jnp.mean(x * x, axis=-1, keepdims=True) + EPS) * w


def _sigmoid(x):
    return 1.0 / (1.0 + jnp.exp(-x))


def _ffn_step(x_ref, nw_ref, wg, wu, wd, o_ref, h_ref, active=None):
    first = pl.program_id(1) == 0
    on = (lambda c: c) if active is None else (lambda c: c & active)

    @pl.when(on(first))
    def _():
        x = x_ref[...]
        h_ref[...] = _rmsnorm(x, nw_ref[...]).astype(BF16)
        o_ref[...] = x

    def body():
        h = h_ref[...]
        g = jnp.dot(h, wg[...], preferred_element_type=F32)
        u = jnp.dot(h, wu[...], preferred_element_type=F32)
        a = (0.5 * g * _sigmoid(g) * u).astype(BF16)
        o_ref[...] += jnp.dot(a, wd[...], preferred_element_type=F32)

    if active is None:
        body()
    else:
        pl.when(active)(body)


def _ffn_first_kernel(x_ref, nw_ref, wg_ref, wu_ref, wd_ref, o_ref, wgo_ref, wuo_ref, wdo_ref, h_ref):
    wgo_ref[...] = wg_ref[...].astype(BF16)
    wuo_ref[...] = wu_ref[...].astype(BF16)
    wdo_ref[...] = wd_ref[...].astype(BF16)
    _ffn_step(x_ref, nw_ref, wgo_ref, wuo_ref, wdo_ref, o_ref, h_ref)


def _ffn_first(x, nw, wg, wu, wd, tm, tf):
    t, d = x.shape
    ff = wg.shape[2]
    return pl.pallas_call(
        _ffn_first_kernel,
        grid=(1, ff // tf),
        in_specs=[
            pl.BlockSpec((tm, d), lambda i, f: (0, 0), pipeline_mode=pl.Buffered(1)),
            pl.BlockSpec((1, d), lambda i, f: (0, 0)),
            pl.BlockSpec((None, d, tf), lambda i, f: (0, 0, f)),
            pl.BlockSpec((None, d, tf), lambda i, f: (0, 0, f)),
            pl.BlockSpec((None, tf, d), lambda i, f: (0, f, 0)),
        ],
        out_specs=[
            pl.BlockSpec((tm, d), lambda i, f: (0, 0)),
            pl.BlockSpec((d, tf), lambda i, f: (0, f)),
            pl.BlockSpec((d, tf), lambda i, f: (0, f)),
            pl.BlockSpec((tf, d), lambda i, f: (f, 0)),
        ],
        out_shape=[jax.ShapeDtypeStruct((tm, d), F32), jax.ShapeDtypeStruct((d, ff), BF16),
                   jax.ShapeDtypeStruct((d, ff), BF16), jax.ShapeDtypeStruct((ff, d), BF16)],
        scratch_shapes=[pltpu.VMEM((tm, d), BF16)],
        compiler_params=_params("arbitrary", "arbitrary"),
        name="ffn_first",
    )(x, nw, wg, wu, wd)


def _ffn_kernel(x_ref, nw_ref, wg_ref, wu_ref, wd_ref, *rest, has_next=False, has_first=False):
    rest = list(rest)
    first_ref = rest.pop(0) if has_first else None
    wn_ref = rest.pop(0) if has_next else None
    o_ref = rest.pop(0)
    wno_ref = rest.pop(0) if has_next else None
    h_ref = rest.pop(0)
    if has_next:
        tn = wno_ref.shape[2]
        for jt in range(wno_ref.shape[0]):
            wno_ref[jt] = wn_ref[:, jt * tn:(jt + 1) * tn].astype(wno_ref.dtype)

    if not has_first:
        _ffn_step(x_ref, nw_ref, wg_ref, wu_ref, wd_ref, o_ref, h_ref)
        return

    (sem,) = rest
    i, f = pl.program_id(0), pl.program_id(1)

    @pl.when((i == 0) & (f == 0))
    def _():
        cp = pltpu.make_async_copy(first_ref, o_ref, sem)
        cp.start()
        cp.wait()

    _ffn_step(x_ref, nw_ref, wg_ref, wu_ref, wd_ref, o_ref, h_ref, active=i > 0)


def _ffn(x, nw, wg, wu, wd, tm, tf, w_next=None, tn=None, first=None):
    t, d = x.shape
    ff = wg.shape[1]
    nt, nf = t // tm, ff // tf
    ftile = (lambda i, f: f) if first is None else (lambda i, f: jnp.where(i == 0, 0, f))
    xtile = (lambda i: i) if first is None else (lambda i: jnp.maximum(i, 1))
    in_specs = [
        pl.BlockSpec((tm, d), lambda i, f: (xtile(i), 0)),
        pl.BlockSpec((1, d), lambda i, f: (0, 0)),
        pl.BlockSpec((d, tf), lambda i, f: (0, ftile(i, f))),
        pl.BlockSpec((d, tf), lambda i, f: (0, ftile(i, f))),
        pl.BlockSpec((tf, d), lambda i, f: (ftile(i, f), 0)),
    ]
    out_specs = pl.BlockSpec((tm, d), lambda i, f: (i, 0))
    out_shape = jax.ShapeDtypeStruct((t, d), F32)
    args = (x, nw, wg, wu, wd)
    scratch = [pltpu.VMEM((tm, d), BF16)]
    if first is not None:
        in_specs.append(pl.BlockSpec(memory_space=pl.ANY))
        args += (first,)
        scratch.append(pltpu.SemaphoreType.DMA(()))
    if w_next is not None:
        _, r, c = w_next.shape
        k = max(kk for kk in range(1, nf + 1) if r % (nt * kk) == 0 and (r // (nt * kk)) % BF16_ROWS == 0)
        rps = r // (nt * k)
        slab = lambda i, f: (0, i * k + jnp.minimum(f, k - 1), 0)
        in_specs.append(pl.BlockSpec((None, rps, c), slab))
        out_specs = [out_specs, pl.BlockSpec((c // tn, rps, tn), slab)]
        out_shape = [out_shape, jax.ShapeDtypeStruct((c // tn, r, tn), BF16)]
        args += (w_next,)
    return pl.pallas_call(
        functools.partial(_ffn_kernel, has_next=w_next is not None, has_first=first is not None),
        grid=(nt, nf),
        in_specs=in_specs,
        out_specs=out_specs,
        out_shape=out_shape,
        scratch_shapes=scratch,
        compiler_params=_params("arbitrary", "arbitrary", vmem=BIG_VMEM_LIMIT),
        name="ffn",
    )(*args)


def _rope_matrix():
    m = np.zeros((2 * ROT_HALF, 3 * LANES), np.float32)
    for lane in range(LANES):
        dd = lane % A_HEAD_DIM
        if dd < ROT_DIM:
            m[dd % ROT_HALF, lane] = 1.0
        if dd < ROT_HALF:
            m[ROT_HALF + dd, LANES + lane] = -1.0
        elif dd < ROT_DIM:
            m[ROT_HALF + dd - ROT_HALF, 2 * LANES + lane] = 1.0
    return np.concatenate([m, m], axis=0)


def _rope_tables(pos_ref, invf_ref, m2_ref, tab_ref):
    for gi in range(tab_ref.shape[0] // LANES):
        ang = invf_ref[...] * pos_ref[gi].astype(F32)
        cs = jnp.concatenate([jnp.cos(ang) - 1.0, jnp.sin(ang)], axis=0)
        hi = cs.astype(BF16)
        lo = (cs - hi.astype(F32)).astype(BF16)
        tab_ref[gi * LANES:(gi + 1) * LANES, :] = lax.dot_general(
            jnp.concatenate([hi, lo], axis=0), m2_ref[...], TN_DIMS, preferred_element_type=F32)


def _rope_tile(r, tab_ref, kinds):
    cs = tab_ref[:, 0:LANES] + 1.0
    s1 = tab_ref[:, LANES:2 * LANES]
    s2 = tab_ref[:, 2 * LANES:3 * LANES]
    out = []
    for gi, kind in enumerate(kinds):
        rg = r[:, gi * LANES:(gi + 1) * LANES]
        if kind is not None:
            rg = rg * cs + pltpu.roll(rg, LANES - ROT_HALF, 1) * s1 + pltpu.roll(rg, ROT_HALF, 1) * s2
        if kind == "q":
            rg = rg * Q_SCALE
        out.append(rg)
    return jnp.concatenate(out, axis=1)


def _swa_block(sink_ref, q_at, kv, first, k_s, v_s, store):
    blk = kv.shape[0]
    low = lax.broadcasted_iota(jnp.int32, (1, LANES), 1) < A_HEAD_DIM
    qi = lax.broadcasted_iota(jnp.int32, (blk, 2 * blk), 0)
    kj = lax.broadcasted_iota(jnp.int32, (blk, 2 * blk), 1)
    allowed = (kj > qi) & (kj <= qi + blk) & (jnp.logical_not(first) | (kj >= blk))

    grp = A_HEADS // A_KV_HEADS
    for g in range(A_KV_HEADS):
        c0 = (g // 2) * LANES
        own = low if g % 2 == 0 else jnp.logical_not(low)
        kc = kv[:, c0:c0 + LANES].astype(F32)
        vc = kv[:, A_KV_WIDTH + c0:A_KV_WIDTH + c0 + LANES].astype(F32)
        k2 = jnp.where(own, kc, pltpu.roll(kc, A_HEAD_DIM, 1)).astype(BF16)
        v2 = jnp.where(own, vc, pltpu.roll(vc, A_HEAD_DIM, 1)).astype(BF16)
        k_s[g, blk:, :] = k2
        v_s[g, blk:, :] = v2
        qs = []
        for pr in range(grp // 2):
            qp = q_at((g * grp + 2 * pr) * A_HEAD_DIM).astype(F32)
            qs += [jnp.where(low, qp, 0.0), jnp.where(low, 0.0, qp)]
        qst = jnp.concatenate(qs, axis=0).astype(BF16)
        s = lax.dot_general(qst, k_s[g], NT_DIMS, preferred_element_type=F32)
        es, rden = [], []
        for r in range(grp):
            sk = sink_ref[g * grp + r] * LOG2E
            sr = jnp.where(allowed, s[r * blk:(r + 1) * blk], -jnp.inf)
            m = jnp.maximum(jnp.max(sr, axis=-1, keepdims=True), sk)
            e = jnp.exp2(sr - m)
            rden.append(1.0 / (jnp.sum(e, axis=-1, keepdims=True) + jnp.exp2(sk - m)))
            es.append(e.astype(BF16))
        o = jnp.dot(jnp.concatenate(es, axis=0), v_s[g], preferred_element_type=F32)
        for pr in range(grp // 2):
            oa = o[(2 * pr) * blk:(2 * pr + 1) * blk] * rden[2 * pr]
            ob = o[(2 * pr + 1) * blk:(2 * pr + 2) * blk] * rden[2 * pr + 1]
            store((g * grp + 2 * pr) * A_HEAD_DIM, jnp.where(low, oa, ob))
        k_s[g, :blk, :] = k2
        v_s[g, :blk, :] = v2


HGRN_CHUNK = 256
HGRN_QUAD = 128


def _hgrn_chunk(a, nw, qp, fp, v, og, st, c_s, kc_s, v_s):
    cn = qp.shape[0]
    nblk = cn // SUBLANES
    quad = min(HGRN_QUAD, cn)
    bpq = quad // SUBLANES
    assert cn in (quad, 2 * quad)

    ea = jnp.exp(a - jnp.max(a, axis=0, keepdims=True))
    lb = ea[0:1] / jnp.sum(ea, axis=0, keepdims=True)

    q = qp * _sigmoid(qp)
    fg = lb + (1.0 - lb) * _sigmoid(fp)
    v_s[...] = v

    row8 = lax.broadcasted_iota(jnp.int32, (cn, LANES), 0) % SUBLANES
    w = jnp.log(fg) * LOG2E
    for sh in (1, 2, 4):
        w = w + jnp.where(row8 >= sh, pltpu.roll(w, sh, 0), 0.0)
    c_s[...] = w
    run = jnp.zeros((SUBLANES, LANES), F32)
    cb = []
    for m in range(nblk):
        cb.append(w[m * SUBLANES:(m + 1) * SUBLANES] + run)
        run = run + c_s[pl.ds(m * SUBLANES + SUBLANES - 1, 1), :]
    c = jnp.concatenate(cb, axis=0)
    c_s[...] = c
    kc = c - jnp.log(1.0 - fg) * LOG2E
    kc_s[...] = kc
    qb = [q[m * SUBLANES:(m + 1) * SUBLANES] for m in range(nblk)]
    kcb = [kc[m * SUBLANES:(m + 1) * SUBLANES] for m in range(nblk)]
    zero = jnp.zeros((SUBLANES, LANES), F32)

    def level(blocks_q, blocks_k, b):
        qt, kt = [], []
        for m in blocks_q:
            anchor = (m * SUBLANES) // (2 * b) * (2 * b) + b
            if m * SUBLANES >= anchor:
                qt.append(qb[m] * jnp.exp2(cb[m] - c_s[pl.ds(anchor - 1, 1), :]))
            else:
                qt.append(zero)
        for m in blocks_k:
            anchor = (m * SUBLANES) // (2 * b) * (2 * b) + b
            if m * SUBLANES < anchor:
                kt.append(jnp.exp2(c_s[pl.ds(anchor - 1, 1), :] - kcb[m]))
            else:
                kt.append(zero)
        return lax.dot_general(jnp.concatenate(qt, axis=0).astype(BF16), jnp.concatenate(kt, axis=0).astype(BF16),
                               NT_DIMS, preferred_element_type=F32)

    blkxor = (lax.broadcasted_iota(jnp.int32, (quad, quad), 0) ^ lax.broadcasted_iota(jnp.int32, (quad, quad), 1))
    sq = []
    for qd in range(cn // quad):
        blocks = list(range(qd * bpq, (qd + 1) * bpq))
        bs = []
        b = SUBLANES
        while b < quad:
            bs.append(b)
            b *= 2
        sc = level(blocks, blocks, bs[-1])
        for b in reversed(bs[:-1]):
            sc = jnp.where(blkxor < 2 * b, level(blocks, blocks, b), sc)
        sq.append(sc.astype(BF16))

    vb = v.astype(BF16)
    if cn == quad:
        o = jnp.dot(sq[0], vb, preferred_element_type=F32)
    else:
        lo_blocks, hi_blocks = list(range(bpq)), list(range(bpq, nblk))
        cross = level(hi_blocks, lo_blocks, quad).astype(BF16)
        o = jnp.concatenate([
            jnp.dot(sq[0], vb[:quad], preferred_element_type=F32),
            jnp.dot(jnp.concatenate([cross, sq[1]], axis=1), vb, preferred_element_type=F32)], axis=0)

    r8 = lax.broadcasted_iota(jnp.int32, (SUBLANES, LANES), 0)
    od = [zero] * nblk
    for j in range(SUBLANES):
        for m in range(nblk):
            row = pl.ds(m * SUBLANES + j, 1)
            dec = jnp.exp2(jnp.where(r8 >= j, cb[m] - kc_s[row, :], -jnp.inf))
            od[m] = od[m] + jnp.sum(qb[m] * dec, axis=-1, keepdims=True) * v_s[row, :]

    o = (o + jnp.concatenate(od, axis=0)
         + lax.dot_general((q * jnp.exp2(c)).astype(BF16), st.astype(BF16), NT_DIMS, preferred_element_type=F32))

    clast = c[cn - 1:cn]
    kl = jnp.exp2(clast - kc).astype(BF16)
    st_next = jnp.exp2(clast) * st + lax.dot_general(vb, kl, TN_DIMS, preferred_element_type=F32)
    return _rmsnorm(o, nw) * (og * _sigmoid(og)), st_next


def _slab_blocks(r, c, nblocks):
    for ncb in range(1, nblocks + 1):
        if nblocks % ncb or c % ncb or r % (nblocks // ncb):
            continue
        br, bc = r // (nblocks // ncb), c // ncb
        if br % BF16_ROWS == 0 and bc % LANES == 0:
            return br, bc, ncb
    return None


def _cast_specs(weights, nt, nsteps):
    ins, outs = [], []
    for wt in weights:
        _, r, c = wt.shape
        k, (br, bc, ncb) = next((kk, _slab_blocks(r, c, nt * kk)) for kk in range(nsteps, 0, -1)
                                if _slab_blocks(r, c, nt * kk))
        blk = functools.partial(lambda i, j, k, ncb: divmod(i * k + jnp.minimum(j, k - 1), ncb), k=k, ncb=ncb)
        ins.append(pl.BlockSpec((None, br, bc), functools.partial(lambda i, j, blk: (0, *blk(i, j)), blk=blk)))
        outs.append(pl.BlockSpec((br, bc), blk))
    return ins, outs


def _mixer_kernel(sink_ref, pos_ref, invf_ref, m2_ref, x_ref, nw_ref, w_ref, lbp_ref, hnw_ref, *rest,
                  plan, n_hin, ncast, nb_seq, nt_seq, u_gate):
    cast_in, rest = rest[:ncast], rest[ncast:]
    g_ref, oa_ref, ob_ref = rest[:3]
    cast_out, rest = rest[3:3 + ncast], rest[3 + ncast:]
    h_ref, tab_ref, qkv_s, hin_s, k_s, v_s, st_ref, c_s, kc_s, v2_s = rest
    i, j = pl.program_id(0), pl.program_id(1)
    tm = x_ref.shape[0]
    n_rope = len(plan)
    cn = c_s.shape[1]

    for w_in_ref, w_out_ref in zip(cast_in, cast_out):
        w_out_ref[...] = w_in_ref[...].astype(w_out_ref.dtype)

    @pl.when((i == 0) & (j == 0))
    def _():
        k_s[...] = jnp.zeros_like(k_s)
        v_s[...] = jnp.zeros_like(v_s)
        st_ref[...] = jnp.zeros_like(st_ref)

    @pl.when(j == 0)
    def _():
        h_ref[...] = _rmsnorm(x_ref[...], nw_ref[...]).astype(BF16)
        _rope_tables(pos_ref, invf_ref, m2_ref, tab_ref)

    def project():
        return jnp.dot(h_ref[...], w_ref[...], preferred_element_type=F32)

    for jj, kinds in enumerate(plan):
        @pl.when(j == jj)
        def _(jj=jj, kinds=kinds):
            qkv_s[jj] = _rope_tile(project(), tab_ref, kinds).astype(BF16)

    hpt = PROJ_TILE // B_DIM
    nchunk = tm // cn
    fresh = i % nt_seq == 0

    @pl.when((j >= n_rope) & (j < n_rope + n_hin))
    def _():
        u = j - n_rope
        r = project().astype(BF16)
        for hh in range(hpt):
            hin_s[u * hpt + hh] = r[:, hh * B_DIM:(hh + 1) * B_DIM]
        rows = pl.ds(pl.multiple_of(u * WINDOW, WINDOW), WINDOW)
        first = (i * n_hin + u) % nb_seq == 0
        q_at = lambda col: qkv_s[col // PROJ_TILE, rows, col % PROJ_TILE:col % PROJ_TILE + LANES]

        def store(col, val):
            oa_ref[rows, col:col + LANES] = val.astype(oa_ref.dtype)

        _swa_block(sink_ref, q_at, qkv_s[n_rope - 1, rows, :], first, k_s, v_s, store)

    @pl.when(j >= n_rope + n_hin)
    def _():
        g_ref[...] = project().astype(g_ref.dtype)
        ids = [(j - n_rope - n_hin) * u_gate + e for e in range(u_gate)]
        heads = [idx // (hpt * nchunk) * hpt + idx % hpt for idx in ids]
        chunks = [idx % (hpt * nchunk) // hpt for idx in ids]
        sts = [jnp.where(fresh & (ck == 0), 0.0, st_ref[hd]) for hd, ck in zip(heads, chunks)]
        for e, (hd, ck) in enumerate(zip(heads, chunks)):
            rows = pl.ds(pl.multiple_of(ck * cn, cn), cn)
            part = lambda pi: hin_s[pi * B_HEADS + hd, rows, :].astype(F32)
            o, sts[e] = _hgrn_chunk(lbp_ref[hd], hnw_ref[hd], part(0), part(1), part(2), part(3), sts[e],
                                    c_s.at[e], kc_s.at[e], v2_s.at[e])
            ob_ref[hd, rows, :] = o.astype(ob_ref.dtype)
        for hd, st in zip(heads, sts):
            st_ref[hd] = st


def _mixers(x, nw, w, pos3, invf8, m2, sinks, lbp, hnw, weights, tm, seq):
    t, d = x.shape
    ntile, _, tn = w.shape
    assert tn == PROJ_TILE and A_WIDTH % tn == 0 and 2 * A_KV_WIDTH == tn and seq % tm == 0
    kind = lambda col: "q" if col < A_WIDTH else ("k" if col < A_WIDTH + A_KV_WIDTH else None)
    plan = [tuple(kind(jj * tn + gi * LANES) for gi in range(tn // LANES)) for jj in range(ntile)]
    plan = plan[:max(jj + 1 for jj, kinds in enumerate(plan) if any(kinds))]
    n_rope = len(plan)
    n_hin = 4 * B_WIDTH // tn
    n_gate = ntile - n_rope - n_hin
    cn = min(HGRN_CHUNK, tm)
    nt = t // tm
    units = B_HEADS * (tm // cn)
    u_gate = units // n_gate
    assert tm // WINDOW == n_hin and n_gate * u_gate == units and u_gate <= tn // B_DIM
    cast_in, cast_out = _cast_specs(weights, nt, ntile)
    const2 = lambda i, j: (0, 0)
    const3 = lambda i, j: (0, 0, 0)
    outs = pl.pallas_call(
        functools.partial(_mixer_kernel, plan=plan, n_hin=n_hin, ncast=len(weights),
                          nb_seq=seq // WINDOW, nt_seq=seq // tm, u_gate=u_gate),
        grid=(nt, ntile),
        in_specs=[
            pl.BlockSpec(memory_space=pltpu.SMEM),
            pl.BlockSpec((tm // LANES, 1, LANES), lambda i, j: (i, 0, 0)),
            pl.BlockSpec(invf8.shape, const2),
            pl.BlockSpec(m2.shape, const2),
            pl.BlockSpec((tm, d), lambda i, j: (i, 0)),
            pl.BlockSpec((1, d), const2),
            pl.BlockSpec((None, d, tn), lambda i, j: (j, 0, 0)),
            pl.BlockSpec(lbp.shape, const3),
            pl.BlockSpec(hnw.shape, const3),
            *cast_in,
        ],
        out_specs=[
            pl.BlockSpec((None, tm, tn), lambda i, j: (jnp.maximum(j - n_rope - n_hin, 0), i, 0)),
            pl.BlockSpec((tm, A_WIDTH), lambda i, j: (i, 0)),
            pl.BlockSpec((B_HEADS, tm, B_DIM), lambda i, j: (0, i, 0)),
            *cast_out,
        ],
        out_shape=[
            jax.ShapeDtypeStruct((n_gate, t, tn), BF16),
            jax.ShapeDtypeStruct((t, A_WIDTH), BF16),
            jax.ShapeDtypeStruct((B_HEADS, t, B_DIM), BF16),
            *[jax.ShapeDtypeStruct(wt.shape[1:], BF16) for wt in weights],
        ],
        scratch_shapes=[
            pltpu.VMEM((tm, d), BF16),
            pltpu.VMEM((tm, 3 * LANES), F32),
            pltpu.VMEM((n_rope, tm, tn), BF16),
            pltpu.VMEM((4 * B_HEADS, tm, B_DIM), BF16),
            pltpu.VMEM((A_KV_HEADS, 2 * WINDOW, LANES), BF16),
            pltpu.VMEM((A_KV_HEADS, 2 * WINDOW, LANES), BF16),
            pltpu.VMEM((B_HEADS, B_DIM, B_DIM), F32),
            pltpu.VMEM((u_gate, cn, B_DIM), F32),
            pltpu.VMEM((u_gate, cn, B_DIM), F32),
            pltpu.VMEM((u_gate, cn, B_DIM), F32),
        ],
        compiler_params=pltpu.CompilerParams(dimension_semantics=("arbitrary", "arbitrary"),
                                             vmem_limit_bytes=BIG_VMEM_LIMIT),
        name="mixers",
    )(sinks, pos3, invf8, m2, x, nw, w, lbp, hnw, *weights)
    return outs[0], outs[1], outs[2], list(outs[3:])


def _merge_kernel(*refs, nchunk):
    x_ref, oa_ref, ob_ref = refs[:3]
    ga_refs, gb_refs = refs[3:3 + nchunk], refs[3 + nchunk:3 + 2 * nchunk]
    wa_ref, wb_ref, wo_ref, o_ref = refs[3 + 2 * nchunk:]
    tn = ga_refs[0].shape[1]
    o_ref[...] = x_ref[...]
    oa = oa_ref[...]
    ob = jnp.concatenate([ob_ref[hd] for hd in range(ob_ref.shape[0])], axis=1)
    for c in range(nchunk):
        sl = slice(c * tn, (c + 1) * tn)
        ma = jnp.dot(oa, wa_ref[:, sl], preferred_element_type=F32)
        mb = jnp.dot(ob, wb_ref[:, sl], preferred_element_type=F32)
        merged = _sigmoid(ga_refs[c][...].astype(F32)) * ma + _sigmoid(gb_refs[c][...].astype(F32)) * mb
        o_ref[...] += jnp.dot(merged.astype(BF16), wo_ref[sl, :], preferred_element_type=F32)


def _merge(x, out_a, out_b, gates, wa, wb, wo, tm):
    t, d = x.shape
    tn = PROJ_TILE
    nchunk = d // tn
    gate = lambda c0: [pl.BlockSpec((None, tm, tn), functools.partial(lambda i, cc: (cc, i, 0), cc=c0 + c))
                       for c in range(nchunk)]
    return pl.pallas_call(
        functools.partial(_merge_kernel, nchunk=nchunk),
        grid=(t // tm,),
        in_specs=[
            pl.BlockSpec((tm, d), lambda i: (i, 0)),
            pl.BlockSpec((tm, A_WIDTH), lambda i: (i, 0)),
            pl.BlockSpec((B_HEADS, tm, B_DIM), lambda i: (0, i, 0)),
            *gate(0), *gate(nchunk),
            _resident(wa.shape), _resident(wb.shape), _resident(wo.shape),
        ],
        out_specs=pl.BlockSpec((tm, d), lambda i: (i, 0)),
        out_shape=jax.ShapeDtypeStruct((t, d), F32),
        compiler_params=_params("parallel"),
        name="merge",
    )(x, out_a, out_b, *([gates] * (2 * nchunk)), wa, wb, wo)


PLE_SUBTILES = 2


def _ple_kernel(x_ref, p_ref, nw_ref, fw_ref, wg32_ref, wp32_ref, o_ref, wg_ref, wp_ref, *, tn):
    @pl.when(pl.program_id(0) == 0)
    def _():
        wg_ref[...] = wg32_ref[0].astype(BF16)
        wp_ref[...] = wp32_ref[0].astype(BF16)

    d = x_ref.shape[1]
    rs = x_ref.shape[0] // PLE_SUBTILES
    for s in range(PLE_SUBTILES):
        rows = slice(s * rs, (s + 1) * rs)
        x = x_ref[rows, :]
        h = _rmsnorm(x, nw_ref[...]).astype(BF16)
        pb = p_ref[rows, :].astype(BF16)
        ss = jnp.zeros((rs, 1), F32)
        for c in range(d // tn):
            sl = slice(c * tn, (c + 1) * tn)
            g = _sigmoid(jnp.dot(h, wg_ref[:, sl], preferred_element_type=F32))
            y = x[:, sl] + g * jnp.dot(pb, wp_ref[:, sl], preferred_element_type=F32)
            ss = ss + jnp.sum(y * y, axis=-1, keepdims=True)
            o_ref[rows, sl] = y
        o_ref[rows, :] = o_ref[rows, :] * lax.rsqrt(ss * (1.0 / d) + EPS) * fw_ref[...]


def _ple(x, p, nw, fw, wg, wp, tm, tn):
    t, d = x.shape
    pd = p.shape[1]
    return pl.pallas_call(
        functools.partial(_ple_kernel, tn=tn),
        grid=(t // tm,),
        in_specs=[
            pl.BlockSpec((tm, d), lambda i: (i, 0)),
            pl.BlockSpec((tm, pd), lambda i: (i, 0)),
            pl.BlockSpec((1, d), lambda i: (0, 0)),
            pl.BlockSpec((1, d), lambda i: (0, 0)),
            _resident(wg.shape), _resident(wp.shape),
        ],
        out_specs=pl.BlockSpec((tm, d), lambda i: (i, 0)),
        out_shape=jax.ShapeDtypeStruct((t, d), F32),
        scratch_shapes=[pltpu.VMEM((d, d), BF16), pltpu.VMEM((pd, d), BF16)],
        compiler_params=_params("arbitrary"),
        name="ple",
    )(x, p, nw, fw, wg, wp)


def _tiles(t, d, ff):
    return dict(ffn_tm=min(1024, t), ffn_tf=min(512, ff), ffn_first_tf=min(256, ff), mix_tm=min(1024, t),
                merge_tm=min(512, t), ple_tm=min(512, t), ple_tn=min(512, d))


def kernel(x, p, positions, ffn1_norm, ffn1_w_gate, ffn1_w_up, ffn1_w_down, mix_norm, w_in, attn_sinks, hgrn_lower_bound, hgrn_norm, w_up_a, w_up_b, w_out, ffn2_norm, ffn2_w_gate, ffn2_w_up, ffn2_w_down, ple_norm, ple_w_gate, ple_w_proj, final_norm):
    batch, seq, d = x.shape
    assert ffn1_norm.shape[0] == 1, "single-layer stack"
    t = batch * seq
    ts = _tiles(t, d, ffn1_w_gate.shape[-1])
    bf = lambda wt: wt.astype(BF16)
    row = lambda wt: wt.reshape(1, -1).astype(F32)

    inv_freq = jnp.power(jnp.float32(ROPE_THETA), -jnp.arange(0, ROT_DIM, 2, dtype=F32) / ROT_DIM)
    invf8 = jnp.broadcast_to(inv_freq[:, None], (ROT_HALF, LANES)).astype(F32)
    m2 = jnp.asarray(_rope_matrix(), BF16)
    pos3 = positions.reshape(t // LANES, 1, LANES)
    lbp = hgrn_lower_bound.astype(F32).reshape(-1, B_HEADS, B_DIM).transpose(1, 0, 2)
    hnw = hgrn_norm[0].astype(F32).reshape(B_HEADS, 1, B_DIM)

    x0 = x.reshape(t, d)
    x1_first, wg1, wu1, wd1 = _ffn_first(x0, row(ffn1_norm[0]), ffn1_w_gate, ffn1_w_up, ffn1_w_down,
                                         ts["ffn_tm"], ts["ffn_first_tf"])
    x1, w_in_t = _ffn(x0, row(ffn1_norm[0]), wg1, wu1, wd1, ts["ffn_tm"], ts["ffn_tf"],
                      w_next=w_in, tn=PROJ_TILE, first=x1_first)
    later = [ffn2_w_gate, ffn2_w_up, ffn2_w_down]
    gates, out_a, out_b, (wg2, wu2, wd2) = _mixers(
        x1, row(mix_norm[0]), w_in_t, pos3, invf8, m2, attn_sinks[0].astype(F32), lbp, hnw, later,
        ts["mix_tm"], seq)
    x2 = _merge(x1, out_a, out_b, gates, bf(w_up_a[0]), bf(w_up_b[0]), bf(w_out[0]), ts["merge_tm"])
    x3 = _ffn(x2, row(ffn2_norm[0]), wg2, wu2, wd2, ts["ffn_tm"], ts["ffn_tf"])
    out = _ple(x3, p[0].reshape(t, -1), row(ple_norm[0]), row(final_norm), ple_w_gate, ple_w_proj,
               ts["ple_tm"], ts["ple_tn"])
    return out.reshape(batch, seq, d)
```

```python
import functools

import jax
import jax.numpy as jnp
import numpy as np
from jax import lax
from jax.experimental import pallas as pl
from jax.experimental.pallas import tpu as pltpu

EPS = 1e-6
A_HEADS = 16
A_KV_HEADS = 4
A_HEAD_DIM = 64
A_WIDTH = A_HEADS * A_HEAD_DIM
A_KV_WIDTH = A_KV_HEADS * A_HEAD_DIM
WINDOW = 128
ROT_DIM = A_HEAD_DIM // 4
ROT_HALF = ROT_DIM // 2
ROPE_THETA = 500000.0
B_HEADS = 8
B_DIM = 128
B_WIDTH = B_HEADS * B_DIM

LANES = 128
SUBLANES = 8
BF16_ROWS = 2 * SUBLANES
PROJ_TILE = 512
VMEM_LIMIT = 56 * 1024 * 1024
BIG_VMEM_LIMIT = 58 * 1024 * 1024
LOG2E = 1.4426950408889634
Q_SCALE = A_HEAD_DIM ** -0.5 * LOG2E

F32 = jnp.float32
BF16 = jnp.bfloat16
NT_DIMS = (((1,), (1,)), ((), ()))
TN_DIMS = (((0,), (0,)), ((), ()))


def _params(*sem, vmem=VMEM_LIMIT):
    return pltpu.CompilerParams(dimension_semantics=sem, vmem_limit_bytes=vmem)


def _resident(shape):
    return pl.BlockSpec(shape, lambda *_: (0,) * len(shape), pipeline_mode=pl.Buffered(1))


def _rmsnorm(x, w):
    return x * lax.rsqrt(jnp.mean(x * x, axis=-1, keepdims=True) + EPS) * w


def _sigmoid(x):
    return 0.5 * jnp.tanh(0.5 * x) + 0.5


def _ffn_step(x_ref, nw_ref, wg, wu, wd, o_ref, h_ref, active=None):
    first = pl.program_id(1) == 0
    on = (lambda c: c) if active is None else (lambda c: c & active)

    @pl.when(on(first))
    def _():
        x = x_ref[...]
        h_ref[...] = _rmsnorm(x, nw_ref[...]).astype(BF16)
        o_ref[...] = x

    def body():
        h = h_ref[...]
        g = jnp.dot(h, wg[...], preferred_element_type=F32)
        u = jnp.dot(h, wu[...], preferred_element_type=F32)
        a = (0.5 * g * _sigmoid(g) * u).astype(BF16)
        o_ref[...] += jnp.dot(a, wd[...], preferred_element_type=F32)

    if active is None:
        body()
    else:
        pl.when(active)(body)


def _ffn_first_kernel(x_ref, nw_ref, wg_ref, wu_ref, wd_ref, o_ref, wgo_ref, wuo_ref, wdo_ref, h_ref):
    wgo_ref[...] = wg_ref[...].astype(BF16)
    wuo_ref[...] = wu_ref[...].astype(BF16)
    wdo_ref[...] = wd_ref[...].astype(BF16)
    _ffn_step(x_ref, nw_ref, wgo_ref, wuo_ref, wdo_ref, o_ref, h_ref)


def _ffn_first(x, nw, wg, wu, wd, tm, tf):
    t, d = x.shape
    ff = wg.shape[2]
    return pl.pallas_call(
        _ffn_first_kernel,
        grid=(1, ff // tf),
        in_specs=[
            pl.BlockSpec((tm, d), lambda i, f: (0, 0), pipeline_mode=pl.Buffered(1)),
            pl.BlockSpec((1, d), lambda i, f: (0, 0)),
            pl.BlockSpec((None, d, tf), lambda i, f: (0, 0, f)),
            pl.BlockSpec((None, d, tf), lambda i, f: (0, 0, f)),
            pl.BlockSpec((None, tf, d), lambda i, f: (0, f, 0)),
        ],
        out_specs=[
            pl.BlockSpec((tm, d), lambda i, f: (0, 0)),
            pl.BlockSpec((d, tf), lambda i, f: (0, f)),
            pl.BlockSpec((d, tf), lambda i, f: (0, f)),
            pl.BlockSpec((tf, d), lambda i, f: (f, 0)),
        ],
        out_shape=[jax.ShapeDtypeStruct((tm, d), F32), jax.ShapeDtypeStruct((d, ff), BF16),
                   jax.ShapeDtypeStruct((d, ff), BF16), jax.ShapeDtypeStruct((ff, d), BF16)],
        scratch_shapes=[pltpu.VMEM((tm, d), BF16)],
        compiler_params=_params("arbitrary", "arbitrary"),
        name="ffn_first",
    )(x, nw, wg, wu, wd)


def _ffn_kernel(x_ref, nw_ref, wg_ref, wu_ref, wd_ref, *rest, has_next=False, has_first=False):
    rest = list(rest)
    first_ref = rest.pop(0) if has_first else None
    wn_ref = rest.pop(0) if has_next else None
    o_ref = rest.pop(0)
    wno_ref = rest.pop(0) if has_next else None
    h_ref = rest.pop(0)
    if has_next:
        tn = wno_ref.shape[2]
        for jt in range(wno_ref.shape[0]):
            wno_ref[jt] = wn_ref[:, jt * tn:(jt + 1) * tn].astype(wno_ref.dtype)

    if not has_first:
        _ffn_step(x_ref, nw_ref, wg_ref, wu_ref, wd_ref, o_ref, h_ref)
        return

    (sem,) = rest
    i, f = pl.program_id(0), pl.program_id(1)

    @pl.when((i == 0) & (f == 0))
    def _():
        cp = pltpu.make_async_copy(first_ref, o_ref, sem)
        cp.start()
        cp.wait()

    _ffn_step(x_ref, nw_ref, wg_ref, wu_ref, wd_ref, o_ref, h_ref, active=i > 0)


def _ffn(x, nw, wg, wu, wd, tm, tf, w_next=None, tn=None, first=None):
    t, d = x.shape
    ff = wg.shape[1]
    nt, nf = t // tm, ff // tf
    ftile = (lambda i, f: f) if first is None else (lambda i, f: jnp.where(i == 0, 0, f))
    xtile = (lambda i: i) if first is None else (lambda i: jnp.maximum(i, 1))
    in_specs = [
        pl.BlockSpec((tm, d), lambda i, f: (xtile(i), 0)),
        pl.BlockSpec((1, d), lambda i, f: (0, 0)),
        pl.BlockSpec((d, tf), lambda i, f: (0, ftile(i, f))),
        pl.BlockSpec((d, tf), lambda i, f: (0, ftile(i, f))),
        pl.BlockSpec((tf, d), lambda i, f: (ftile(i, f), 0)),
    ]
    out_specs = pl.BlockSpec((tm, d), lambda i, f: (i, 0))
    out_shape = jax.ShapeDtypeStruct((t, d), F32)
    args = (x, nw, wg, wu, wd)
    scratch = [pltpu.VMEM((tm, d), BF16)]
    if first is not None:
        in_specs.append(pl.BlockSpec(memory_space=pl.ANY))
        args += (first,)
        scratch.append(pltpu.SemaphoreType.DMA(()))
    if w_next is not None:
        _, r, c = w_next.shape
        k = max(kk for kk in range(1, nf + 1) if r % (nt * kk) == 0 and (r // (nt * kk)) % BF16_ROWS == 0)
        rps = r // (nt * k)
        slab = lambda i, f: (0, i * k + jnp.minimum(f, k - 1), 0)
        in_specs.append(pl.BlockSpec((None, rps, c), slab))
        out_specs = [out_specs, pl.BlockSpec((c // tn, rps, tn), slab)]
        out_shape = [out_shape, jax.ShapeDtypeStruct((c // tn, r, tn), BF16)]
        args += (w_next,)
    return pl.pallas_call(
        functools.partial(_ffn_kernel, has_next=w_next is not None, has_first=first is not None),
        grid=(nt, nf),
        in_specs=in_specs,
        out_specs=out_specs,
        out_shape=out_shape,
        scratch_shapes=scratch,
        compiler_params=_params("arbitrary", "arbitrary", vmem=BIG_VMEM_LIMIT),
        name="ffn",
    )(*args)


def _rope_matrix():
    m = np.zeros((2 * ROT_HALF, 3 * LANES), np.float32)
    for lane in range(LANES):
        dd = lane % A_HEAD_DIM
        if dd < ROT_DIM:
            m[dd % ROT_HALF, lane] = 1.0
        if dd < ROT_HALF:
            m[ROT_HALF + dd, LANES + lane] = -1.0
        elif dd < ROT_DIM:
            m[ROT_HALF + dd - ROT_HALF, 2 * LANES + lane] = 1.0
    return np.concatenate([m, m], axis=0)


def _rope_tables(pos_ref, invf_ref, m2_ref, tab_ref):
    for gi in range(tab_ref.shape[0] // LANES):
        ang = invf_ref[...] * pos_ref[gi].astype(F32)
        cs = jnp.concatenate([jnp.cos(ang) - 1.0, jnp.sin(ang)], axis=0)
        hi = cs.astype(BF16)
        lo = (cs - hi.astype(F32)).astype(BF16)
        tab_ref[gi * LANES:(gi + 1) * LANES, :] = lax.dot_general(
            jnp.concatenate([hi, lo], axis=0), m2_ref[...], TN_DIMS, preferred_element_type=F32)


def _rope_tile(r, tab_ref, kinds):
    cs = tab_ref[:, 0:LANES] + 1.0
    s1 = tab_ref[:, LANES:2 * LANES]
    s2 = tab_ref[:, 2 * LANES:3 * LANES]
    out = []
    for gi, kind in enumerate(kinds):
        rg = r[:, gi * LANES:(gi + 1) * LANES]
        if kind is not None:
            rg = rg * cs + pltpu.roll(rg, LANES - ROT_HALF, 1) * s1 + pltpu.roll(rg, ROT_HALF, 1) * s2
        if kind == "q":
            rg = rg * Q_SCALE
        out.append(rg)
    return jnp.concatenate(out, axis=1)


def _swa_block(sink_ref, q_at, kv, first, k_s, v_s, store):
    blk = kv.shape[0]
    low = lax.broadcasted_iota(jnp.int32, (1, LANES), 1) < A_HEAD_DIM
    qi = lax.broadcasted_iota(jnp.int32, (blk, 2 * blk), 0)
    kj = lax.broadcasted_iota(jnp.int32, (blk, 2 * blk), 1)
    allowed = (kj > qi) & (kj <= qi + blk) & (jnp.logical_not(first) | (kj >= blk))

    grp = A_HEADS // A_KV_HEADS
    for g in range(A_KV_HEADS):
        c0 = (g // 2) * LANES
        own = low if g % 2 == 0 else jnp.logical_not(low)
        kc = kv[:, c0:c0 + LANES].astype(F32)
        vc = kv[:, A_KV_WIDTH + c0:A_KV_WIDTH + c0 + LANES].astype(F32)
        k2 = jnp.where(own, kc, pltpu.roll(kc, A_HEAD_DIM, 1)).astype(BF16)
        v2 = jnp.where(own, vc, pltpu.roll(vc, A_HEAD_DIM, 1)).astype(BF16)
        k_s[g, blk:, :] = k2
        v_s[g, blk:, :] = v2
        qs = []
        for pr in range(grp // 2):
            qp = q_at((g * grp + 2 * pr) * A_HEAD_DIM).astype(F32)
            qs += [jnp.where(low, qp, 0.0), jnp.where(low, 0.0, qp)]
        qst = jnp.concatenate(qs, axis=0).astype(BF16)
        s = lax.dot_general(qst, k_s[g], NT_DIMS, preferred_element_type=F32)
        es, rden = [], []
        for r in range(grp):
            sk = sink_ref[g * grp + r] * LOG2E
            sr = jnp.where(allowed, s[r * blk:(r + 1) * blk], -jnp.inf)
            m = jnp.maximum(jnp.max(sr, axis=-1, keepdims=True), sk)
            e = jnp.exp2(sr - m)
            rden.append(1.0 / (jnp.sum(e, axis=-1, keepdims=True) + jnp.exp2(sk - m)))
            es.append(e.astype(BF16))
        o = jnp.dot(jnp.concatenate(es, axis=0), v_s[g], preferred_element_type=F32)
        for pr in range(grp // 2):
            oa = o[(2 * pr) * blk:(2 * pr + 1) * blk] * rden[2 * pr]
            ob = o[(2 * pr + 1) * blk:(2 * pr + 2) * blk] * rden[2 * pr + 1]
            store((g * grp + 2 * pr) * A_HEAD_DIM, jnp.where(low, oa, ob))
        k_s[g, :blk, :] = k2
        v_s[g, :blk, :] = v2


HGRN_CHUNK = 256
HGRN_QUAD = 128


def _hgrn_chunk(a, nw, qp, fp, v, og, st, c_s, kc_s, v_s):
    cn = qp.shape[0]
    nblk = cn // SUBLANES
    quad = min(HGRN_QUAD, cn)
    bpq = quad // SUBLANES
    assert cn in (quad, 2 * quad)

    ea = jnp.exp(a - jnp.max(a, axis=0, keepdims=True))
    lb = ea[0:1] / jnp.sum(ea, axis=0, keepdims=True)

    q = qp * _sigmoid(qp)
    fg = lb + (1.0 - lb) * _sigmoid(fp)
    v_s[...] = v

    row8 = lax.broadcasted_iota(jnp.int32, (cn, LANES), 0) % SUBLANES
    w = jnp.log(fg) * LOG2E
    for sh in (1, 2, 4):
        w = w + jnp.where(row8 >= sh, pltpu.roll(w, sh, 0), 0.0)
    c_s[...] = w
    run = jnp.zeros((SUBLANES, LANES), F32)
    cb = []
    for m in range(nblk):
        cb.append(w[m * SUBLANES:(m + 1) * SUBLANES] + run)
        run = run + c_s[pl.ds(m * SUBLANES + SUBLANES - 1, 1), :]
    c = jnp.concatenate(cb, axis=0)
    c_s[...] = c
    kc = c - jnp.log(1.0 - fg) * LOG2E
    kc_s[...] = kc
    qb = [q[m * SUBLANES:(m + 1) * SUBLANES] for m in range(nblk)]
    kcb = [kc[m * SUBLANES:(m + 1) * SUBLANES] for m in range(nblk)]
    zero = jnp.zeros((SUBLANES, LANES), F32)

    def level(blocks_q, blocks_k, b):
        qt, kt = [], []
        for m in blocks_q:
            anchor = (m * SUBLANES) // (2 * b) * (2 * b) + b
            if m * SUBLANES >= anchor:
                qt.append(qb[m] * jnp.exp2(cb[m] - c_s[pl.ds(anchor - 1, 1), :]))
            else:
                qt.append(zero)
        for m in blocks_k:
            anchor = (m * SUBLANES) // (2 * b) * (2 * b) + b
            if m * SUBLANES < anchor:
                kt.append(jnp.exp2(c_s[pl.ds(anchor - 1, 1), :] - kcb[m]))
            else:
                kt.append(zero)
        return lax.dot_general(jnp.concatenate(qt, axis=0).astype(BF16), jnp.concatenate(kt, axis=0).astype(BF16),
                               NT_DIMS, preferred_element_type=F32)

    blkxor = (lax.broadcasted_iota(jnp.int32, (quad, quad), 0) ^ lax.broadcasted_iota(jnp.int32, (quad, quad), 1))
    sq = []
    for qd in range(cn // quad):
        blocks = list(range(qd * bpq, (qd + 1) * bpq))
        bs = []
        b = SUBLANES
        while b < quad:
            bs.append(b)
            b *= 2
        sc = level(blocks, blocks, bs[-1])
        for b in reversed(bs[:-1]):
            sc = jnp.where(blkxor < 2 * b, level(blocks, blocks, b), sc)
        sq.append(sc.astype(BF16))

    vb = v.astype(BF16)
    if cn == quad:
        o = jnp.dot(sq[0], vb, preferred_element_type=F32)
    else:
        lo_blocks, hi_blocks = list(range(bpq)), list(range(bpq, nblk))
        cross = level(hi_blocks, lo_blocks, quad).astype(BF16)
        o = jnp.concatenate([
            jnp.dot(sq[0], vb[:quad], preferred_element_type=F32),
            jnp.dot(jnp.concatenate([cross, sq[1]], axis=1), vb, preferred_element_type=F32)], axis=0)

    r8 = lax.broadcasted_iota(jnp.int32, (SUBLANES, LANES), 0)
    od = [zero] * nblk
    for j in range(SUBLANES):
        for m in range(nblk):
            row = pl.ds(m * SUBLANES + j, 1)
            dec = jnp.exp2(jnp.where(r8 >= j, cb[m] - kc_s[row, :], -jnp.inf))
            od[m] = od[m] + jnp.sum(qb[m] * dec, axis=-1, keepdims=True) * v_s[row, :]

    o = (o + jnp.concatenate(od, axis=0)
         + lax.dot_general((q * jnp.exp2(c)).astype(BF16), st.astype(BF16), NT_DIMS, preferred_element_type=F32))

    clast = c[cn - 1:cn]
    kl = jnp.exp2(clast - kc).astype(BF16)
    st_next = jnp.exp2(clast) * st + lax.dot_general(vb, kl, TN_DIMS, preferred_element_type=F32)
    return _rmsnorm(o, nw) * (og * _sigmoid(og)), st_next


def _slab_blocks(r, c, nblocks):
    for ncb in range(1, nblocks + 1):
        if nblocks % ncb or c % ncb or r % (nblocks // ncb):
            continue
        br, bc = r // (nblocks // ncb), c // ncb
        if br % BF16_ROWS == 0 and bc % LANES == 0:
            return br, bc, ncb
    return None


def _cast_specs(weights, nt, nsteps):
    ins, outs = [], []
    for wt in weights:
        _, r, c = wt.shape
        k, (br, bc, ncb) = next((kk, _slab_blocks(r, c, nt * kk)) for kk in range(nsteps, 0, -1)
                                if _slab_blocks(r, c, nt * kk))
        blk = functools.partial(lambda i, j, k, ncb: divmod(i * k + jnp.minimum(j, k - 1), ncb), k=k, ncb=ncb)
        ins.append(pl.BlockSpec((None, br, bc), functools.partial(lambda i, j, blk: (0, *blk(i, j)), blk=blk)))
        outs.append(pl.BlockSpec((br, bc), blk))
    return ins, outs


def _mixer_kernel(sink_ref, pos_ref, invf_ref, m2_ref, x_ref, nw_ref, w_ref, lbp_ref, hnw_ref, *rest,
                  plan, n_hin, ncast, nb_seq, nt_seq, u_gate):
    cast_in, rest = rest[:ncast], rest[ncast:]
    g_ref, oa_ref, ob_ref = rest[:3]
    cast_out, rest = rest[3:3 + ncast], rest[3 + ncast:]
    h_ref, tab_ref, qkv_s, hin_s, k_s, v_s, st_ref, c_s, kc_s, v2_s = rest
    i, j = pl.program_id(0), pl.program_id(1)
    tm = x_ref.shape[0]
    n_rope = len(plan)
    cn = c_s.shape[1]

    for w_in_ref, w_out_ref in zip(cast_in, cast_out):
        w_out_ref[...] = w_in_ref[...].astype(w_out_ref.dtype)

    @pl.when((i == 0) & (j == 0))
    def _():
        k_s[...] = jnp.zeros_like(k_s)
        v_s[...] = jnp.zeros_like(v_s)
        st_ref[...] = jnp.zeros_like(st_ref)

    @pl.when(j == 0)
    def _():
        h_ref[...] = _rmsnorm(x_ref[...], nw_ref[...]).astype(BF16)
        _rope_tables(pos_ref, invf_ref, m2_ref, tab_ref)

    def project():
        return jnp.dot(h_ref[...], w_ref[...], preferred_element_type=F32)

    for jj, kinds in enumerate(plan):
        @pl.when(j == jj)
        def _(jj=jj, kinds=kinds):
            qkv_s[jj] = _rope_tile(project(), tab_ref, kinds).astype(BF16)

    hpt = PROJ_TILE // B_DIM
    nchunk = tm // cn
    fresh = i % nt_seq == 0

    @pl.when((j >= n_rope) & (j < n_rope + n_hin))
    def _():
        u = j - n_rope
        r = project().astype(BF16)
        for hh in range(hpt):
            hin_s[u * hpt + hh] = r[:, hh * B_DIM:(hh + 1) * B_DIM]
        rows = pl.ds(pl.multiple_of(u * WINDOW, WINDOW), WINDOW)
        first = (i * n_hin + u) % nb_seq == 0
        q_at = lambda col: qkv_s[col // PROJ_TILE, rows, col % PROJ_TILE:col % PROJ_TILE + LANES]

        def store(col, val):
            oa_ref[rows, col:col + LANES] = val.astype(oa_ref.dtype)

        _swa_block(sink_ref, q_at, qkv_s[n_rope - 1, rows, :], first, k_s, v_s, store)

    @pl.when(j >= n_rope + n_hin)
    def _():
        g_ref[...] = project().astype(g_ref.dtype)
        ids = [(j - n_rope - n_hin) * u_gate + e for e in range(u_gate)]
        heads = [idx // (hpt * nchunk) * hpt + idx % hpt for idx in ids]
        chunks = [idx % (hpt * nchunk) // hpt for idx in ids]
        sts = [jnp.where(fresh & (ck == 0), 0.0, st_ref[hd]) for hd, ck in zip(heads, chunks)]
        for e, (hd, ck) in enumerate(zip(heads, chunks)):
            rows = pl.ds(pl.multiple_of(ck * cn, cn), cn)
            part = lambda pi: hin_s[pi * B_HEADS + hd, rows, :].astype(F32)
            o, sts[e] = _hgrn_chunk(lbp_ref[hd], hnw_ref[hd], part(0), part(1), part(2), part(3), sts[e],
                                    c_s.at[e], kc_s.at[e], v2_s.at[e])
            ob_ref[hd, rows, :] = o.astype(ob_ref.dtype)
        for hd, st in zip(heads, sts):
            st_ref[hd] = st


def _mixers(x, nw, w, pos3, invf8, m2, sinks, lbp, hnw, weights, tm, seq):
    t, d = x.shape
    ntile, _, tn = w.shape
    assert tn == PROJ_TILE and A_WIDTH % tn == 0 and 2 * A_KV_WIDTH == tn and seq % tm == 0
    kind = lambda col: "q" if col < A_WIDTH else ("k" if col < A_WIDTH + A_KV_WIDTH else None)
    plan = [tuple(kind(jj * tn + gi * LANES) for gi in range(tn // LANES)) for jj in range(ntile)]
    plan = plan[:max(jj + 1 for jj, kinds in enumerate(plan) if any(kinds))]
    n_rope = len(plan)
    n_hin = 4 * B_WIDTH // tn
    n_gate = ntile - n_rope - n_hin
    cn = min(HGRN_CHUNK, tm)
    nt = t // tm
    units = B_HEADS * (tm // cn)
    u_gate = units // n_gate
    assert tm // WINDOW == n_hin and n_gate * u_gate == units and u_gate <= tn // B_DIM
    cast_in, cast_out = _cast_specs(weights, nt, ntile)
    const2 = lambda i, j: (0, 0)
    const3 = lambda i, j: (0, 0, 0)
    outs = pl.pallas_call(
        functools.partial(_mixer_kernel, plan=plan, n_hin=n_hin, ncast=len(weights),
                          nb_seq=seq // WINDOW, nt_seq=seq // tm, u_gate=u_gate),
        grid=(nt, ntile),
        in_specs=[
            pl.BlockSpec(memory_space=pltpu.SMEM),
            pl.BlockSpec((tm // LANES, 1, LANES), lambda i, j: (i, 0, 0)),
            pl.BlockSpec(invf8.shape, const2),
            pl.BlockSpec(m2.shape, const2),
            pl.BlockSpec((tm, d), lambda i, j: (i, 0)),
            pl.BlockSpec((1, d), const2),
            pl.BlockSpec((None, d, tn), lambda i, j: (j, 0, 0)),
            pl.BlockSpec(lbp.shape, const3),
            pl.BlockSpec(hnw.shape, const3),
            *cast_in,
        ],
        out_specs=[
            pl.BlockSpec((None, tm, tn), lambda i, j: (jnp.maximum(j - n_rope - n_hin, 0), i, 0)),
            pl.BlockSpec((tm, A_WIDTH), lambda i, j: (i, 0)),
            pl.BlockSpec((B_HEADS, tm, B_DIM), lambda i, j: (0, i, 0)),
            *cast_out,
        ],
        out_shape=[
            jax.ShapeDtypeStruct((n_gate, t, tn), BF16),
            jax.ShapeDtypeStruct((t, A_WIDTH), BF16),
            jax.ShapeDtypeStruct((B_HEADS, t, B_DIM), BF16),
            *[jax.ShapeDtypeStruct(wt.shape[1:], BF16) for wt in weights],
        ],
        scratch_shapes=[
            pltpu.VMEM((tm, d), BF16),
            pltpu.VMEM((tm, 3 * LANES), F32),
            pltpu.VMEM((n_rope, tm, tn), BF16),
            pltpu.VMEM((4 * B_HEADS, tm, B_DIM), BF16),
            pltpu.VMEM((A_KV_HEADS, 2 * WINDOW, LANES), BF16),
            pltpu.VMEM((A_KV_HEADS, 2 * WINDOW, LANES), BF16),
            pltpu.VMEM((B_HEADS, B_DIM, B_DIM), F32),
            pltpu.VMEM((u_gate, cn, B_DIM), F32),
            pltpu.VMEM((u_gate, cn, B_DIM), F32),
            pltpu.VMEM((u_gate, cn, B_DIM), F32),
        ],
        compiler_params=pltpu.CompilerParams(dimension_semantics=("arbitrary", "arbitrary"),
                                             vmem_limit_bytes=BIG_VMEM_LIMIT),
        name="mixers",
    )(sinks, pos3, invf8, m2, x, nw, w, lbp, hnw, *weights)
    return outs[0], outs[1], outs[2], list(outs[3:])


def _merge_kernel(*refs, nchunk):
    x_ref, oa_ref, ob_ref = refs[:3]
    ga_refs, gb_refs = refs[3:3 + nchunk], refs[3 + nchunk:3 + 2 * nchunk]
    wa_ref, wb_ref, wo_ref, o_ref = refs[3 + 2 * nchunk:]
    tn = ga_refs[0].shape[1]
    o_ref[...] = x_ref[...]
    oa = oa_ref[...]
    ob = jnp.concatenate([ob_ref[hd] for hd in range(ob_ref.shape[0])], axis=1)
    for c in range(nchunk):
        sl = slice(c * tn, (c + 1) * tn)
        ma = jnp.dot(oa, wa_ref[:, sl], preferred_element_type=F32)
        mb = jnp.dot(ob, wb_ref[:, sl], preferred_element_type=F32)
        merged = _sigmoid(ga_refs[c][...].astype(F32)) * ma + _sigmoid(gb_refs[c][...].astype(F32)) * mb
        o_ref[...] += jnp.dot(merged.astype(BF16), wo_ref[sl, :], preferred_element_type=F32)


def _merge(x, out_a, out_b, gates, wa, wb, wo, tm):
    t, d = x.shape
    tn = PROJ_TILE
    nchunk = d // tn
    gate = lambda c0: [pl.BlockSpec((None, tm, tn), functools.partial(lambda i, cc: (cc, i, 0), cc=c0 + c))
                       for c in range(nchunk)]
    return pl.pallas_call(
        functools.partial(_merge_kernel, nchunk=nchunk),
        grid=(t // tm,),
        in_specs=[
            pl.BlockSpec((tm, d), lambda i: (i, 0)),
            pl.BlockSpec((tm, A_WIDTH), lambda i: (i, 0)),
            pl.BlockSpec((B_HEADS, tm, B_DIM), lambda i: (0, i, 0)),
            *gate(0), *gate(nchunk),
            _resident(wa.shape), _resident(wb.shape), _resident(wo.shape),
        ],
        out_specs=pl.BlockSpec((tm, d), lambda i: (i, 0)),
        out_shape=jax.ShapeDtypeStruct((t, d), F32),
        compiler_params=_params("parallel"),
        name="merge",
    )(x, out_a, out_b, *([gates] * (2 * nchunk)), wa, wb, wo)


PLE_SUBTILES = 2


def _ple_kernel(x_ref, p_ref, nw_ref, fw_ref, wg32_ref, wp32_ref, o_ref, wg_ref, wp_ref, *, tn):
    @pl.when(pl.program_id(0) == 0)
    def _():
        wg_ref[...] = wg32_ref[0].astype(BF16)
        wp_ref[...] = wp32_ref[0].astype(BF16)

    d = x_ref.shape[1]
    rs = x_ref.shape[0] // PLE_SUBTILES
    for s in range(PLE_SUBTILES):
        rows = slice(s * rs, (s + 1) * rs)
        x = x_ref[rows, :]
        h = _rmsnorm(x, nw_ref[...]).astype(BF16)
        pb = p_ref[rows, :].astype(BF16)
        ss = jnp.zeros((rs, 1), F32)
        for c in range(d // tn):
            sl = slice(c * tn, (c + 1) * tn)
            g = _sigmoid(jnp.dot(h, wg_ref[:, sl], preferred_element_type=F32))
            y = x[:, sl] + g * jnp.dot(pb, wp_ref[:, sl], preferred_element_type=F32)
            ss = ss + jnp.sum(y * y, axis=-1, keepdims=True)
            o_ref[rows, sl] = y
        o_ref[rows, :] = o_ref[rows, :] * lax.rsqrt(ss * (1.0 / d) + EPS) * fw_ref[...]


def _ple(x, p, nw, fw, wg, wp, tm, tn):
    t, d = x.shape
    pd = p.shape[1]
    return pl.pallas_call(
        functools.partial(_ple_kernel, tn=tn),
        grid=(t // tm,),
        in_specs=[
            pl.BlockSpec((tm, d), lambda i: (i, 0)),
            pl.BlockSpec((tm, pd), lambda i: (i, 0)),
            pl.BlockSpec((1, d), lambda i: (0, 0)),
            pl.BlockSpec((1, d), lambda i: (0, 0)),
            _resident(wg.shape), _resident(wp.shape),
        ],
        out_specs=pl.BlockSpec((tm, d), lambda i: (i, 0)),
        out_shape=jax.ShapeDtypeStruct((t, d), F32),
        scratch_shapes=[pltpu.VMEM((d, d), BF16), pltpu.VMEM((pd, d), BF16)],
        compiler_params=_params("arbitrary"),
        name="ple",
    )(x, p, nw, fw, wg, wp)


def _tiles(t, d, ff):
    return dict(ffn_tm=min(1024, t), ffn_tf=min(512, ff), ffn_first_tf=min(256, ff), mix_tm=min(1024, t),
                merge_tm=min(512, t), ple_tm=min(512, t), ple_tn=min(512, d))


def kernel(x, p, positions, ffn1_norm, ffn1_w_gate, ffn1_w_up, ffn1_w_down, mix_norm, w_in, attn_sinks, hgrn_lower_bound, hgrn_norm, w_up_a, w_up_b, w_out, ffn2_norm, ffn2_w_gate, ffn2_w_up, ffn2_w_down, ple_norm, ple_w_gate, ple_w_proj, final_norm):
    batch, seq, d = x.shape
    assert ffn1_norm.shape[0] == 1, "single-layer stack"
    t = batch * seq
    ts = _tiles(t, d, ffn1_w_gate.shape[-1])
    bf = lambda wt: wt.astype(BF16)
    row = lambda wt: wt.reshape(1, -1).astype(F32)

    inv_freq = jnp.power(jnp.float32(ROPE_THETA), -jnp.arange(0, ROT_DIM, 2, dtype=F32) / ROT_DIM)
    invf8 = jnp.broadcast_to(inv_freq[:, None], (ROT_HALF, LANES)).astype(F32)
    m2 = jnp.asarray(_rope_matrix(), BF16)
    pos3 = positions.reshape(t // LANES, 1, LANES)
    lbp = hgrn_lower_bound.astype(F32).reshape(-1, B_HEADS, B_DIM).transpose(1, 0, 2)
    hnw = hgrn_norm[0].astype(F32).reshape(B_HEADS, 1, B_DIM)

    x0 = x.reshape(t, d)
    x1_first, wg1, wu1, wd1 = _ffn_first(x0, row(ffn1_norm[0]), ffn1_w_gate, ffn1_w_up, ffn1_w_down,
                                         ts["ffn_tm"], ts["ffn_first_tf"])
    x1, w_in_t = _ffn(x0, row(ffn1_norm[0]), wg1, wu1, wd1, ts["ffn_tm"], ts["ffn_tf"],
                      w_next=w_in, tn=PROJ_TILE, first=x1_first)
    later = [ffn2_w_gate, ffn2_w_up, ffn2_w_down]
    gates, out_a, out_b, (wg2, wu2, wd2) = _mixers(
        x1, row(mix_norm[0]), w_in_t, pos3, invf8, m2, attn_sinks[0].astype(F32), lbp, hnw, later,
        ts["mix_tm"], seq)
    x2 = _merge(x1, out_a, out_b, gates, bf(w_up_a[0]), bf(w_up_b[0]), bf(w_out[0]), ts["merge_tm"])
    x3 = _ffn(x2, row(ffn2_norm[0]), wg2, wu2, wd2, ts["ffn_tm"], ts["ffn_tf"])
    out = _ple(x3, p[0].reshape(t, -1), row(ple_norm[0]), row(final_norm), ple_w_gate, ple_w_proj,
               ts["ple_tm"], ts["ple_tn"])
    return out.reshape(batch, seq, d)
```

```python
import functools

import jax
import jax.numpy as jnp
import numpy as np
from jax import lax
from jax.experimental import pallas as pl
from jax.experimental.pallas import tpu as pltpu

EPS = 1e-6
A_HEADS = 16
A_KV_HEADS = 4
A_HEAD_DIM = 64
A_WIDTH = A_HEADS * A_HEAD_DIM
A_KV_WIDTH = A_KV_HEADS * A_HEAD_DIM
WINDOW = 128
ROT_DIM = A_HEAD_DIM // 4
ROT_HALF = ROT_DIM // 2
ROPE_THETA = 500000.0
B_HEADS = 8
B_DIM = 128
B_WIDTH = B_HEADS * B_DIM

LANES = 128
SUBLANES = 8
BF16_ROWS = 2 * SUBLANES
PROJ_TILE = 512
VMEM_LIMIT = 56 * 1024 * 1024
BIG_VMEM_LIMIT = 58 * 1024 * 1024
LOG2E = 1.4426950408889634
Q_SCALE = A_HEAD_DIM ** -0.5 * LOG2E

F32 = jnp.float32
BF16 = jnp.bfloat16
NT_DIMS = (((1,), (1,)), ((), ()))
TN_DIMS = (((0,), (0,)), ((), ()))


def _params(*sem, vmem=VMEM_LIMIT):
    return pltpu.CompilerParams(dimension_semantics=sem, vmem_limit_bytes=vmem)


def _resident(shape):
    return pl.BlockSpec(shape, lambda *_: (0,) * len(shape), pipeline_mode=pl.Buffered(1))


def _rmsnorm(x, w):
    return x * lax.rsqrt(jnp.mean(x * x, axis=-1, keepdims=True) + EPS) * w


def _sigmoid(x):
    return 0.5 * jnp.tanh(0.5 * x) + 0.5


def _silu(x):
    h = 0.5 * x
    return h * jnp.tanh(h) + h


def _ffn_step(x_ref, nw_ref, wg, wu, wd, o_ref, h_ref, active=None):
    first = pl.program_id(1) == 0
    on = (lambda c: c) if active is None else (lambda c: c & active)

    @pl.when(on(first))
    def _():
        x = x_ref[...]
        h_ref[...] = _rmsnorm(x, nw_ref[...]).astype(BF16)
        o_ref[...] = x

    def body():
        h = h_ref[...]
        g = jnp.dot(h, wg[...], preferred_element_type=F32)
        u = jnp.dot(h, wu[...], preferred_element_type=F32)
        a = (0.5 * g * _sigmoid(g) * u).astype(BF16)
        o_ref[...] += jnp.dot(a, wd[...], preferred_element_type=F32)

    if active is None:
        body()
    else:
        pl.when(active)(body)


def _ffn_first_kernel(x_ref, nw_ref, wg_ref, wu_ref, wd_ref, o_ref, wgo_ref, wuo_ref, wdo_ref, h_ref):
    wgo_ref[...] = wg_ref[...].astype(BF16)
    wuo_ref[...] = wu_ref[...].astype(BF16)
    wdo_ref[...] = wd_ref[...].astype(BF16)
    _ffn_step(x_ref, nw_ref, wgo_ref, wuo_ref, wdo_ref, o_ref, h_ref)


def _ffn_first(x, nw, wg, wu, wd, tm, tf):
    t, d = x.shape
    ff = wg.shape[2]
    return pl.pallas_call(
        _ffn_first_kernel,
        grid=(1, ff // tf),
        in_specs=[
            pl.BlockSpec((tm, d), lambda i, f: (0, 0), pipeline_mode=pl.Buffered(1)),
            pl.BlockSpec((1, d), lambda i, f: (0, 0)),
            pl.BlockSpec((None, d, tf), lambda i, f: (0, 0, f)),
            pl.BlockSpec((None, d, tf), lambda i, f: (0, 0, f)),
            pl.BlockSpec((None, tf, d), lambda i, f: (0, f, 0)),
        ],
        out_specs=[
            pl.BlockSpec((tm, d), lambda i, f: (0, 0)),
            pl.BlockSpec((d, tf), lambda i, f: (0, f)),
            pl.BlockSpec((d, tf), lambda i, f: (0, f)),
            pl.BlockSpec((tf, d), lambda i, f: (f, 0)),
        ],
        out_shape=[jax.ShapeDtypeStruct((tm, d), F32), jax.ShapeDtypeStruct((d, ff), BF16),
                   jax.ShapeDtypeStruct((d, ff), BF16), jax.ShapeDtypeStruct((ff, d), BF16)],
        scratch_shapes=[pltpu.VMEM((tm, d), BF16)],
        compiler_params=_params("arbitrary", "arbitrary"),
        name="ffn_first",
    )(x, nw, wg, wu, wd)


def _ffn_kernel(x_ref, nw_ref, wg_ref, wu_ref, wd_ref, *rest, has_next=False, has_first=False):
    rest = list(rest)
    first_ref = rest.pop(0) if has_first else None
    wn_ref = rest.pop(0) if has_next else None
    o_ref = rest.pop(0)
    wno_ref = rest.pop(0) if has_next else None
    h_ref = rest.pop(0)
    if has_next:
        tn = wno_ref.shape[2]
        for jt in range(wno_ref.shape[0]):
            wno_ref[jt] = wn_ref[:, jt * tn:(jt + 1) * tn].astype(wno_ref.dtype)

    if not has_first:
        _ffn_step(x_ref, nw_ref, wg_ref, wu_ref, wd_ref, o_ref, h_ref)
        return

    (sem,) = rest
    i, f = pl.program_id(0), pl.program_id(1)

    @pl.when((i == 0) & (f == 0))
    def _():
        cp = pltpu.make_async_copy(first_ref, o_ref, sem)
        cp.start()
        cp.wait()

    _ffn_step(x_ref, nw_ref, wg_ref, wu_ref, wd_ref, o_ref, h_ref, active=i > 0)


def _ffn(x, nw, wg, wu, wd, tm, tf, w_next=None, tn=None, first=None):
    t, d = x.shape
    ff = wg.shape[1]
    nt, nf = t // tm, ff // tf
    ftile = (lambda i, f: f) if first is None else (lambda i, f: jnp.where(i == 0, 0, f))
    xtile = (lambda i: i) if first is None else (lambda i: jnp.maximum(i, 1))
    in_specs = [
        pl.BlockSpec((tm, d), lambda i, f: (xtile(i), 0)),
        pl.BlockSpec((1, d), lambda i, f: (0, 0)),
        pl.BlockSpec((d, tf), lambda i, f: (0, ftile(i, f))),
        pl.BlockSpec((d, tf), lambda i, f: (0, ftile(i, f))),
        pl.BlockSpec((tf, d), lambda i, f: (ftile(i, f), 0)),
    ]
    out_specs = pl.BlockSpec((tm, d), lambda i, f: (i, 0))
    out_shape = jax.ShapeDtypeStruct((t, d), F32)
    args = (x, nw, wg, wu, wd)
    scratch = [pltpu.VMEM((tm, d), BF16)]
    if first is not None:
        in_specs.append(pl.BlockSpec(memory_space=pl.ANY))
        args += (first,)
        scratch.append(pltpu.SemaphoreType.DMA(()))
    if w_next is not None:
        _, r, c = w_next.shape
        k = max(kk for kk in range(1, nf + 1) if r % (nt * kk) == 0 and (r // (nt * kk)) % BF16_ROWS == 0)
        rps = r // (nt * k)
        slab = lambda i, f: (0, i * k + jnp.minimum(f, k - 1), 0)
        in_specs.append(pl.BlockSpec((None, rps, c), slab))
        out_specs = [out_specs, pl.BlockSpec((c // tn, rps, tn), slab)]
        out_shape = [out_shape, jax.ShapeDtypeStruct((c // tn, r, tn), BF16)]
        args += (w_next,)
    return pl.pallas_call(
        functools.partial(_ffn_kernel, has_next=w_next is not None, has_first=first is not None),
        grid=(nt, nf),
        in_specs=in_specs,
        out_specs=out_specs,
        out_shape=out_shape,
        scratch_shapes=scratch,
        compiler_params=_params("arbitrary", "arbitrary", vmem=BIG_VMEM_LIMIT),
        name="ffn",
    )(*args)


def _rope_matrix():
    m = np.zeros((2 * ROT_HALF, 3 * LANES), np.float32)
    for lane in range(LANES):
        dd = lane % A_HEAD_DIM
        if dd < ROT_DIM:
            m[dd % ROT_HALF, lane] = 1.0
        if dd < ROT_HALF:
            m[ROT_HALF + dd, LANES + lane] = -1.0
        elif dd < ROT_DIM:
            m[ROT_HALF + dd - ROT_HALF, 2 * LANES + lane] = 1.0
    return np.concatenate([m, m], axis=0)


def _rope_tables(pos_ref, invf_ref, m2_ref, tab_ref):
    for gi in range(tab_ref.shape[0] // LANES):
        ang = invf_ref[...] * pos_ref[gi].astype(F32)
        cs = jnp.concatenate([jnp.cos(ang) - 1.0, jnp.sin(ang)], axis=0)
        hi = cs.astype(BF16)
        lo = (cs - hi.astype(F32)).astype(BF16)
        tab_ref[gi * LANES:(gi + 1) * LANES, :] = lax.dot_general(
            jnp.concatenate([hi, lo], axis=0), m2_ref[...], TN_DIMS, preferred_element_type=F32)


def _rope_tile(r, tab_ref, kinds):
    cs = tab_ref[:, 0:LANES] + 1.0
    s1 = tab_ref[:, LANES:2 * LANES]
    s2 = tab_ref[:, 2 * LANES:3 * LANES]
    out = []
    for gi, kind in enumerate(kinds):
        rg = r[:, gi * LANES:(gi + 1) * LANES]
        if kind is not None:
            rg = rg * cs + pltpu.roll(rg, LANES - ROT_HALF, 1) * s1 + pltpu.roll(rg, ROT_HALF, 1) * s2
        if kind == "q":
            rg = rg * Q_SCALE
        out.append(rg)
    return jnp.concatenate(out, axis=1)


def _swa_block(sink_ref, q_at, kv, first, k_s, v_s, store):
    blk = kv.shape[0]
    low = lax.broadcasted_iota(jnp.int32, (1, LANES), 1) < A_HEAD_DIM
    qi = lax.broadcasted_iota(jnp.int32, (blk, 2 * blk), 0)
    kj = lax.broadcasted_iota(jnp.int32, (blk, 2 * blk), 1)
    allowed = (kj > qi) & (kj <= qi + blk) & (jnp.logical_not(first) | (kj >= blk))

    grp = A_HEADS // A_KV_HEADS
    for g in range(A_KV_HEADS):
        c0 = (g // 2) * LANES
        own = low if g % 2 == 0 else jnp.logical_not(low)
        kc = kv[:, c0:c0 + LANES].astype(F32)
        vc = kv[:, A_KV_WIDTH + c0:A_KV_WIDTH + c0 + LANES].astype(F32)
        k2 = jnp.where(own, kc, pltpu.roll(kc, A_HEAD_DIM, 1)).astype(BF16)
        v2 = jnp.where(own, vc, pltpu.roll(vc, A_HEAD_DIM, 1)).astype(BF16)
        k_s[g, blk:, :] = k2
        v_s[g, blk:, :] = v2
        qs = []
        for pr in range(grp // 2):
            qp = q_at((g * grp + 2 * pr) * A_HEAD_DIM).astype(F32)
            qs += [jnp.where(low, qp, 0.0), jnp.where(low, 0.0, qp)]
        qst = jnp.concatenate(qs, axis=0).astype(BF16)
        s = lax.dot_general(qst, k_s[g], NT_DIMS, preferred_element_type=F32)
        es, rden = [], []
        for r in range(grp):
            sk = sink_ref[g * grp + r] * LOG2E
            sr = jnp.where(allowed, s[r * blk:(r + 1) * blk], -jnp.inf)
            m = jnp.maximum(jnp.max(sr, axis=-1, keepdims=True), sk)
            e = jnp.exp2(sr - m)
            rden.append(1.0 / (jnp.sum(e, axis=-1, keepdims=True) + jnp.exp2(sk - m)))
            es.append(e.astype(BF16))
        o = jnp.dot(jnp.concatenate(es, axis=0), v_s[g], preferred_element_type=F32)
        for pr in range(grp // 2):
            oa = o[(2 * pr) * blk:(2 * pr + 1) * blk] * rden[2 * pr]
            ob = o[(2 * pr + 1) * blk:(2 * pr + 2) * blk] * rden[2 * pr + 1]
            store((g * grp + 2 * pr) * A_HEAD_DIM, jnp.where(low, oa, ob))
        k_s[g, :blk, :] = k2
        v_s[g, :blk, :] = v2


HGRN_CHUNK = 256
HGRN_QUAD = 128


def _hgrn_chunk(a, nw, qp, fp, v, og, st, c_s, kc_s, v_s):
    cn = qp.shape[0]
    nblk = cn // SUBLANES
    quad = min(HGRN_QUAD, cn)
    bpq = quad // SUBLANES
    assert cn in (quad, 2 * quad)

    ea = jnp.exp(a - jnp.max(a, axis=0, keepdims=True))
    lb = ea[0:1] / jnp.sum(ea, axis=0, keepdims=True)

    q = _silu(qp)
    fg = lb + (1.0 - lb) * _sigmoid(fp)
    v_s[...] = v

    row8 = lax.broadcasted_iota(jnp.int32, (cn, LANES), 0) % SUBLANES
    w = jnp.log(fg) * LOG2E
    for sh in (1, 2, 4):
        w = w + jnp.where(row8 >= sh, pltpu.roll(w, sh, 0), 0.0)
    c_s[...] = w
    run = jnp.zeros((SUBLANES, LANES), F32)
    cb = []
    for m in range(nblk):
        cb.append(w[m * SUBLANES:(m + 1) * SUBLANES] + run)
        run = run + c_s[pl.ds(m * SUBLANES + SUBLANES - 1, 1), :]
    c = jnp.concatenate(cb, axis=0)
    c_s[...] = c
    kc = c - jnp.log(1.0 - fg) * LOG2E
    kc_s[...] = kc
    qb = [q[m * SUBLANES:(m + 1) * SUBLANES] for m in range(nblk)]
    kcb = [kc[m * SUBLANES:(m + 1) * SUBLANES] for m in range(nblk)]
    zero = jnp.zeros((SUBLANES, LANES), F32)

    def level(blocks_q, blocks_k, b):
        qt, kt = [], []
        for m in blocks_q:
            anchor = (m * SUBLANES) // (2 * b) * (2 * b) + b
            if m * SUBLANES >= anchor:
                qt.append(qb[m] * jnp.exp2(cb[m] - c_s[pl.ds(anchor - 1, 1), :]))
            else:
                qt.append(zero)
        for m in blocks_k:
            anchor = (m * SUBLANES) // (2 * b) * (2 * b) + b
            if m * SUBLANES < anchor:
                kt.append(jnp.exp2(c_s[pl.ds(anchor - 1, 1), :] - kcb[m]))
            else:
                kt.append(zero)
        return lax.dot_general(jnp.concatenate(qt, axis=0).astype(BF16), jnp.concatenate(kt, axis=0).astype(BF16),
                               NT_DIMS, preferred_element_type=F32)

    blkxor = (lax.broadcasted_iota(jnp.int32, (quad, quad), 0) ^ lax.broadcasted_iota(jnp.int32, (quad, quad), 1))
    sq = []
    for qd in range(cn // quad):
        blocks = list(range(qd * bpq, (qd + 1) * bpq))
        bs = []
        b = SUBLANES
        while b < quad:
            bs.append(b)
            b *= 2
        sc = level(blocks, blocks, bs[-1])
        for b in reversed(bs[:-1]):
            sc = jnp.where(blkxor < 2 * b, level(blocks, blocks, b), sc)
        sq.append(sc.astype(BF16))

    vb = v.astype(BF16)
    if cn == quad:
        o = jnp.dot(sq[0], vb, preferred_element_type=F32)
    else:
        lo_blocks, hi_blocks = list(range(bpq)), list(range(bpq, nblk))
        cross = level(hi_blocks, lo_blocks, quad).astype(BF16)
        o = jnp.concatenate([
            jnp.dot(sq[0], vb[:quad], preferred_element_type=F32),
            jnp.dot(jnp.concatenate([cross, sq[1]], axis=1), vb, preferred_element_type=F32)], axis=0)

    r8 = lax.broadcasted_iota(jnp.int32, (SUBLANES, LANES), 0)
    od = [zero] * nblk
    for j in range(SUBLANES):
        for m in range(nblk):
            row = pl.ds(m * SUBLANES + j, 1)
            dec = jnp.exp2(jnp.where(r8 >= j, cb[m] - kc_s[row, :], -jnp.inf))
            od[m] = od[m] + jnp.sum(qb[m] * dec, axis=-1, keepdims=True) * v_s[row, :]

    o = (o + jnp.concatenate(od, axis=0)
         + lax.dot_general((q * jnp.exp2(c)).astype(BF16), st.astype(BF16), NT_DIMS, preferred_element_type=F32))

    clast = c[cn - 1:cn]
    kl = jnp.exp2(clast - kc).astype(BF16)
    st_next = jnp.exp2(clast) * st + lax.dot_general(vb, kl, TN_DIMS, preferred_element_type=F32)
    return _rmsnorm(o, nw) * _silu(og), st_next


def _slab_blocks(r, c, nblocks):
    for ncb in range(1, nblocks + 1):
        if nblocks % ncb or c % ncb or r % (nblocks // ncb):
            continue
        br, bc = r // (nblocks // ncb), c // ncb
        if br % BF16_ROWS == 0 and bc % LANES == 0:
            return br, bc, ncb
    return None


def _cast_specs(weights, nt, nsteps):
    ins, outs = [], []
    for wt in weights:
        _, r, c = wt.shape
        k, (br, bc, ncb) = next((kk, _slab_blocks(r, c, nt * kk)) for kk in range(nsteps, 0, -1)
                                if _slab_blocks(r, c, nt * kk))
        blk = functools.partial(lambda i, j, k, ncb: divmod(i * k + jnp.minimum(j, k - 1), ncb), k=k, ncb=ncb)
        ins.append(pl.BlockSpec((None, br, bc), functools.partial(lambda i, j, blk: (0, *blk(i, j)), blk=blk)))
        outs.append(pl.BlockSpec((br, bc), blk))
    return ins, outs


def _mixer_kernel(sink_ref, pos_ref, invf_ref, m2_ref, x_ref, nw_ref, w_ref, lbp_ref, hnw_ref, *rest,
                  plan, n_hin, ncast, nb_seq, nt_seq, u_gate):
    cast_in, rest = rest[:ncast], rest[ncast:]
    g_ref, oa_ref, ob_ref = rest[:3]
    cast_out, rest = rest[3:3 + ncast], rest[3 + ncast:]
    h_ref, tab_ref, qkv_s, hin_s, k_s, v_s, st_ref, c_s, kc_s, v2_s = rest
    i, j = pl.program_id(0), pl.program_id(1)
    tm = x_ref.shape[0]
    n_rope = len(plan)
    cn = c_s.shape[1]

    for w_in_ref, w_out_ref in zip(cast_in, cast_out):
        w_out_ref[...] = w_in_ref[...].astype(w_out_ref.dtype)

    @pl.when((i == 0) & (j == 0))
    def _():
        k_s[...] = jnp.zeros_like(k_s)
        v_s[...] = jnp.zeros_like(v_s)
        st_ref[...] = jnp.zeros_like(st_ref)

    @pl.when(j == 0)
    def _():
        h_ref[...] = _rmsnorm(x_ref[...], nw_ref[...]).astype(BF16)
        _rope_tables(pos_ref, invf_ref, m2_ref, tab_ref)

    def project():
        return jnp.dot(h_ref[...], w_ref[...], preferred_element_type=F32)

    for jj, kinds in enumerate(plan):
        @pl.when(j == jj)
        def _(jj=jj, kinds=kinds):
            qkv_s[jj] = _rope_tile(project(), tab_ref, kinds).astype(BF16)

    hpt = PROJ_TILE // B_DIM
    nchunk = tm // cn
    fresh = i % nt_seq == 0

    @pl.when((j >= n_rope) & (j < n_rope + n_hin))
    def _():
        u = j - n_rope
        r = project().astype(BF16)
        for hh in range(hpt):
            hin_s[u * hpt + hh] = r[:, hh * B_DIM:(hh + 1) * B_DIM]
        rows = pl.ds(pl.multiple_of(u * WINDOW, WINDOW), WINDOW)
        first = (i * n_hin + u) % nb_seq == 0
        q_at = lambda col: qkv_s[col // PROJ_TILE, rows, col % PROJ_TILE:col % PROJ_TILE + LANES]

        def store(col, val):
            oa_ref[rows, col:col + LANES] = val.astype(oa_ref.dtype)

        _swa_block(sink_ref, q_at, qkv_s[n_rope - 1, rows, :], first, k_s, v_s, store)

    @pl.when(j >= n_rope + n_hin)
    def _():
        g_ref[...] = project().astype(g_ref.dtype)
        ids = [(j - n_rope - n_hin) * u_gate + e for e in range(u_gate)]
        heads = [idx // (hpt * nchunk) * hpt + idx % hpt for idx in ids]
        chunks = [idx % (hpt * nchunk) // hpt for idx in ids]
        sts = [jnp.where(fresh & (ck == 0), 0.0, st_ref[hd]) for hd, ck in zip(heads, chunks)]
        for e, (hd, ck) in enumerate(zip(heads, chunks)):
            rows = pl.ds(pl.multiple_of(ck * cn, cn), cn)
            part = lambda pi: hin_s[pi * B_HEADS + hd, rows, :].astype(F32)
            o, sts[e] = _hgrn_chunk(lbp_ref[hd], hnw_ref[hd], part(0), part(1), part(2), part(3), sts[e],
                                    c_s.at[e], kc_s.at[e], v2_s.at[e])
            ob_ref[hd, rows, :] = o.astype(ob_ref.dtype)
        for hd, st in zip(heads, sts):
            st_ref[hd] = st


def _mixers(x, nw, w, pos3, invf8, m2, sinks, lbp, hnw, weights, tm, seq):
    t, d = x.shape
    ntile, _, tn = w.shape
    assert tn == PROJ_TILE and A_WIDTH % tn == 0 and 2 * A_KV_WIDTH == tn and seq % tm == 0
    kind = lambda col: "q" if col < A_WIDTH else ("k" if col < A_WIDTH + A_KV_WIDTH else None)
    plan = [tuple(kind(jj * tn + gi * LANES) for gi in range(tn // LANES)) for jj in range(ntile)]
    plan = plan[:max(jj + 1 for jj, kinds in enumerate(plan) if any(kinds))]
    n_rope = len(plan)
    n_hin = 4 * B_WIDTH // tn
    n_gate = ntile - n_rope - n_hin
    cn = min(HGRN_CHUNK, tm)
    nt = t // tm
    units = B_HEADS * (tm // cn)
    u_gate = units // n_gate
    assert tm // WINDOW == n_hin and n_gate * u_gate == units and u_gate <= tn // B_DIM
    cast_in, cast_out = _cast_specs(weights, nt, ntile)
    const2 = lambda i, j: (0, 0)
    const3 = lambda i, j: (0, 0, 0)
    outs = pl.pallas_call(
        functools.partial(_mixer_kernel, plan=plan, n_hin=n_hin, ncast=len(weights),
                          nb_seq=seq // WINDOW, nt_seq=seq // tm, u_gate=u_gate),
        grid=(nt, ntile),
        in_specs=[
            pl.BlockSpec(memory_space=pltpu.SMEM),
            pl.BlockSpec((tm // LANES, 1, LANES), lambda i, j: (i, 0, 0)),
            pl.BlockSpec(invf8.shape, const2),
            pl.BlockSpec(m2.shape, const2),
            pl.BlockSpec((tm, d), lambda i, j: (i, 0)),
            pl.BlockSpec((1, d), const2),
            pl.BlockSpec((None, d, tn), lambda i, j: (j, 0, 0)),
            pl.BlockSpec(lbp.shape, const3),
            pl.BlockSpec(hnw.shape, const3),
            *cast_in,
        ],
        out_specs=[
            pl.BlockSpec((None, tm, tn), lambda i, j: (jnp.maximum(j - n_rope - n_hin, 0), i, 0)),
            pl.BlockSpec((tm, A_WIDTH), lambda i, j: (i, 0)),
            pl.BlockSpec((B_HEADS, tm, B_DIM), lambda i, j: (0, i, 0)),
            *cast_out,
        ],
        out_shape=[
            jax.ShapeDtypeStruct((n_gate, t, tn), BF16),
            jax.ShapeDtypeStruct((t, A_WIDTH), BF16),
            jax.ShapeDtypeStruct((B_HEADS, t, B_DIM), BF16),
            *[jax.ShapeDtypeStruct(wt.shape[1:], BF16) for wt in weights],
        ],
        scratch_shapes=[
            pltpu.VMEM((tm, d), BF16),
            pltpu.VMEM((tm, 3 * LANES), F32),
            pltpu.VMEM((n_rope, tm, tn), BF16),
            pltpu.VMEM((4 * B_HEADS, tm, B_DIM), BF16),
            pltpu.VMEM((A_KV_HEADS, 2 * WINDOW, LANES), BF16),
            pltpu.VMEM((A_KV_HEADS, 2 * WINDOW, LANES), BF16),
            pltpu.VMEM((B_HEADS, B_DIM, B_DIM), F32),
            pltpu.VMEM((u_gate, cn, B_DIM), F32),
            pltpu.VMEM((u_gate, cn, B_DIM), F32),
            pltpu.VMEM((u_gate, cn, B_DIM), F32),
        ],
        compiler_params=pltpu.CompilerParams(dimension_semantics=("arbitrary", "arbitrary"),
                                             vmem_limit_bytes=BIG_VMEM_LIMIT),
        name="mixers",
    )(sinks, pos3, invf8, m2, x, nw, w, lbp, hnw, *weights)
    return outs[0], outs[1], outs[2], list(outs[3:])


def _merge_kernel(*refs, nchunk):
    x_ref, oa_ref, ob_ref = refs[:3]
    ga_refs, gb_refs = refs[3:3 + nchunk], refs[3 + nchunk:3 + 2 * nchunk]
    wa_ref, wb_ref, wo_ref, o_ref = refs[3 + 2 * nchunk:]
    tn = ga_refs[0].shape[1]
    o_ref[...] = x_ref[...]
    oa = oa_ref[...]
    ob = jnp.concatenate([ob_ref[hd] for hd in range(ob_ref.shape[0])], axis=1)
    for c in range(nchunk):
        sl = slice(c * tn, (c + 1) * tn)
        ma = jnp.dot(oa, wa_ref[:, sl], preferred_element_type=F32)
        mb = jnp.dot(ob, wb_ref[:, sl], preferred_element_type=F32)
        merged = _sigmoid(ga_refs[c][...].astype(F32)) * ma + _sigmoid(gb_refs[c][...].astype(F32)) * mb
        o_ref[...] += jnp.dot(merged.astype(BF16), wo_ref[sl, :], preferred_element_type=F32)


def _merge(x, out_a, out_b, gates, wa, wb, wo, tm):
    t, d = x.shape
    tn = PROJ_TILE
    nchunk = d // tn
    gate = lambda c0: [pl.BlockSpec((None, tm, tn), functools.partial(lambda i, cc: (cc, i, 0), cc=c0 + c))
                       for c in range(nchunk)]
    return pl.pallas_call(
        functools.partial(_merge_kernel, nchunk=nchunk),
        grid=(t // tm,),
        in_specs=[
            pl.BlockSpec((tm, d), lambda i: (i, 0)),
            pl.BlockSpec((tm, A_WIDTH), lambda i: (i, 0)),
            pl.BlockSpec((B_HEADS, tm, B_DIM), lambda i: (0, i, 0)),
            *gate(0), *gate(nchunk),
            _resident(wa.shape), _resident(wb.shape), _resident(wo.shape),
        ],
        out_specs=pl.BlockSpec((tm, d), lambda i: (i, 0)),
        out_shape=jax.ShapeDtypeStruct((t, d), F32),
        compiler_params=_params("parallel"),
        name="merge",
    )(x, out_a, out_b, *([gates] * (2 * nchunk)), wa, wb, wo)


PLE_SUBTILES = 2


def _ple_kernel(x_ref, p_ref, nw_ref, fw_ref, wg32_ref, wp32_ref, o_ref, wg_ref, wp_ref, *, tn):
    @pl.when(pl.program_id(0) == 0)
    def _():
        wg_ref[...] = wg32_ref[0].astype(BF16)
        wp_ref[...] = wp32_ref[0].astype(BF16)

    d = x_ref.shape[1]
    rs = x_ref.shape[0] // PLE_SUBTILES
    for s in range(PLE_SUBTILES):
        rows = slice(s * rs, (s + 1) * rs)
        x = x_ref[rows, :]
        h = _rmsnorm(x, nw_ref[...]).astype(BF16)
        pb = p_ref[rows, :].astype(BF16)
        ss = jnp.zeros((rs, 1), F32)
        for c in range(d // tn):
            sl = slice(c * tn, (c + 1) * tn)
            g = _sigmoid(jnp.dot(h, wg_ref[:, sl], preferred_element_type=F32))
            y = x[:, sl] + g * jnp.dot(pb, wp_ref[:, sl], preferred_element_type=F32)
            ss = ss + jnp.sum(y * y, axis=-1, keepdims=True)
            o_ref[rows, sl] = y
        o_ref[rows, :] = o_ref[rows, :] * lax.rsqrt(ss * (1.0 / d) + EPS) * fw_ref[...]


def _ple(x, p, nw, fw, wg, wp, tm, tn):
    t, d = x.shape
    pd = p.shape[1]
    return pl.pallas_call(
        functools.partial(_ple_kernel, tn=tn),
        grid=(t // tm,),
        in_specs=[
            pl.BlockSpec((tm, d), lambda i: (i, 0)),
            pl.BlockSpec((tm, pd), lambda i: (i, 0)),
            pl.BlockSpec((1, d), lambda i: (0, 0)),
            pl.BlockSpec((1, d), lambda i: (0, 0)),
            _resident(wg.shape), _resident(wp.shape),
        ],
        out_specs=pl.BlockSpec((tm, d), lambda i: (i, 0)),
        out_shape=jax.ShapeDtypeStruct((t, d), F32),
        scratch_shapes=[pltpu.VMEM((d, d), BF16), pltpu.VMEM((pd, d), BF16)],
        compiler_params=_params("arbitrary"),
        name="ple",
    )(x, p, nw, fw, wg, wp)


def _tiles(t, d, ff):
    return dict(ffn_tm=min(1024, t), ffn_tf=min(512, ff), ffn_first_tf=min(256, ff), mix_tm=min(1024, t),
                merge_tm=min(512, t), ple_tm=min(512, t), ple_tn=min(512, d))


def kernel(x, p, positions, ffn1_norm, ffn1_w_gate, ffn1_w_up, ffn1_w_down, mix_norm, w_in, attn_sinks, hgrn_lower_bound, hgrn_norm, w_up_a, w_up_b, w_out, ffn2_norm, ffn2_w_gate, ffn2_w_up, ffn2_w_down, ple_norm, ple_w_gate, ple_w_proj, final_norm):
    batch, seq, d = x.shape
    assert ffn1_norm.shape[0] == 1, "single-layer stack"
    t = batch * seq
    ts = _tiles(t, d, ffn1_w_gate.shape[-1])
    bf = lambda wt: wt.astype(BF16)
    row = lambda wt: wt.reshape(1, -1).astype(F32)

    inv_freq = jnp.power(jnp.float32(ROPE_THETA), -jnp.arange(0, ROT_DIM, 2, dtype=F32) / ROT_DIM)
    invf8 = jnp.broadcast_to(inv_freq[:, None], (ROT_HALF, LANES)).astype(F32)
    m2 = jnp.asarray(_rope_matrix(), BF16)
    pos3 = positions.reshape(t // LANES, 1, LANES)
    lbp = hgrn_lower_bound.astype(F32).reshape(-1, B_HEADS, B_DIM).transpose(1, 0, 2)
    hnw = hgrn_norm[0].astype(F32).reshape(B_HEADS, 1, B_DIM)

    x0 = x.reshape(t, d)
    x1_first, wg1, wu1, wd1 = _ffn_first(x0, row(ffn1_norm[0]), ffn1_w_gate, ffn1_w_up, ffn1_w_down,
                                         ts["ffn_tm"], ts["ffn_first_tf"])
    x1, w_in_t = _ffn(x0, row(ffn1_norm[0]), wg1, wu1, wd1, ts["ffn_tm"], ts["ffn_tf"],
                      w_next=w_in, tn=PROJ_TILE, first=x1_first)
    later = [ffn2_w_gate, ffn2_w_up, ffn2_w_down]
    gates, out_a, out_b, (wg2, wu2, wd2) = _mixers(
        x1, row(mix_norm[0]), w_in_t, pos3, invf8, m2, attn_sinks[0].astype(F32), lbp, hnw, later,
        ts["mix_tm"], seq)
    x2 = _merge(x1, out_a, out_b, gates, bf(w_up_a[0]), bf(w_up_b[0]), bf(w_out[0]), ts["merge_tm"])
    x3 = _ffn(x2, row(ffn2_norm[0]), wg2, wu2, wd2, ts["ffn_tm"], ts["ffn_tf"])
    out = _ple(x3, p[0].reshape(t, -1), row(ple_norm[0]), row(final_norm), ple_w_gate, ple_w_proj,
               ts["ple_tm"], ts["ple_tn"])
    return out.reshape(batch, seq, d)
```
